```python
import math
import jax, jax.numpy as jnp
from jax import lax
import numpy as np

D_MODEL = 1024
BATCH = 16
SEQ = 4096
DEPTH = 1

CHUNK = 64
PLE_DIM = 256
ROPE_THETA = 10000.0
NORM_EPS = 1e-6

GLA_HEADS = 4
GLA_WIDTH = D_MODEL // 2
GLA_DK = GLA_WIDTH // 2 // GLA_HEADS
GLA_DV = GLA_WIDTH // GLA_HEADS
GLA_GATE_RANK = 16
GLA_GATE_NORM = 16.0

DIFF_HEADS = 4
DIFF_WIDTH = D_MODEL - GLA_WIDTH
DIFF_DV = DIFF_WIDTH // DIFF_HEADS
DIFF_DQK = DIFF_DV // 2
Q_BLOCK = 128

D_FF = 256 * ((8 * D_MODEL // 3 + 255) // 256)
CONV_W = 3

IN_SIZES = (GLA_HEADS * GLA_DK, GLA_HEADS * GLA_DK, GLA_WIDTH, GLA_WIDTH, GLA_GATE_RANK,
            DIFF_WIDTH, DIFF_WIDTH, DIFF_WIDTH)
IN_COLS = sum(IN_SIZES)

kernel_name = "hymba_gla_diffattn_convffn_ple"


def rms_norm(x, g):
    xf = x.astype(jnp.float32)
    y = xf * lax.rsqrt(jnp.mean(xf * xf, axis=-1, keepdims=True) + NORM_EPS)
    return (y * g.astype(jnp.float32)).astype(x.dtype)


def rope(x, pos):
    d = x.shape[-1]
    inv_freq = ROPE_THETA ** (-jnp.arange(0, d, 2, dtype=jnp.float32) / d)
    ang = pos.astype(jnp.float32)[..., None] * inv_freq
    cos, sin = jnp.cos(ang)[:, :, None, :], jnp.sin(ang)[:, :, None, :]
    xf = x.astype(jnp.float32)
    x1, x2 = xf[..., : d // 2], xf[..., d // 2:]
    return jnp.concatenate([x1 * cos - x2 * sin, x2 * cos + x1 * sin], axis=-1).astype(x.dtype)


def gla_mixer(q, k, v, g, a_low, w_a_up, b_a, norm_g):
    B, T = q.shape[:2]
    N = T // CHUNK
    H, dk, dv = GLA_HEADS, GLA_DK, GLA_DV
    qf = q.astype(jnp.float32).reshape(B, N, CHUNK, H, dk) * (dk ** -0.5)
    kf = k.astype(jnp.float32).reshape(B, N, CHUNK, H, dk)
    vf = v.astype(jnp.float32).reshape(B, N, CHUNK, H, dv)
    log_a = jax.nn.log_sigmoid((a_low @ w_a_up + b_a).astype(jnp.float32)) / GLA_GATE_NORM
    b = jnp.cumsum(log_a.reshape(B, N, CHUNK, H, dk), axis=2)
    b_last = b[:, :, -1:]
    eb, enb = jnp.exp(b), jnp.exp(-b)
    q_fwd = qf * eb
    a_fwd = jnp.einsum('bnthd,bnshd->bnhts', q_fwd, kf * enb)
    a_bwd = jnp.einsum('bnthd,bnshd->bnhts', qf * enb, kf * eb)
    tri = jnp.tril(jnp.ones((CHUNK, CHUNK), dtype=bool))
    a = jnp.where(tri, a_fwd, a_bwd)
    o_intra = jnp.einsum('bnhts,bnshv->bnthv', a, vf)
    d_state = jnp.einsum('bnshk,bnshv->bnhkv', kf * jnp.exp(b_last - b), vf)
    decay = jnp.exp(b_last[:, :, 0])

    def step(state, inp):
        dec, ds = inp
        return dec[..., None] * state + ds, state

    s0 = jnp.zeros((B, H, dk, dv), jnp.float32)
    _, s_prev = lax.scan(step, s0, (jnp.moveaxis(decay, 1, 0), jnp.moveaxis(d_state, 1, 0)))
    s_prev = jnp.moveaxis(s_prev, 0, 1)
    o_inter = jnp.einsum('bnthk,bnhkv->bnthv', q_fwd, s_prev)
    o = (o_intra + o_inter).reshape(B, T, H, dv)
    o = rms_norm(o, norm_g.reshape(H, dv)).reshape(B, T, H * dv)
    return (o * jax.nn.silu(g.astype(jnp.float32))).astype(q.dtype)


def diff_attention(q, k, v, pos, lam, lam_init, norm_g):
    B, T = q.shape[:2]
    H, dqk, dv = DIFF_HEADS, DIFF_DQK, DIFF_DV
    q = (rope(q.reshape(B, T, H * 2, dqk), pos) * (dqk ** -0.5)).reshape(B, T, H, 2, dqk)
    k = rope(k.reshape(B, T, H * 2, dqk), pos).reshape(B, T, H, 2, dqk)
    v = v.reshape(B, T, H, dv)
    outs = []
    for blk in range(T // Q_BLOCK):
        q0 = blk * Q_BLOCK
        k_end = q0 + Q_BLOCK
        s = jnp.einsum('bqhmd,bkhmd->bhmqk', q[:, q0:k_end], k[:, :k_end]).astype(jnp.float32)
        q_chunk = (q0 + jnp.arange(Q_BLOCK)) // CHUNK
        k_chunk = jnp.arange(k_end) // CHUNK
        mask = k_chunk[None, :] <= q_chunk[:, None]
        s = jnp.where(mask, s, jnp.float32(-1e30))
        pr = jax.nn.softmax(s, axis=-1)
        w = pr[:, :, 0] - lam * pr[:, :, 1]
        outs.append(jnp.einsum('bhqk,bkhd->bqhd', w.astype(v.dtype), v[:, :k_end]))
    o = jnp.concatenate(outs, axis=1)
    o = rms_norm(o, norm_g.reshape(H, dv)) * (1.0 - lam_init)
    return o.reshape(B, T, H * dv)


def conv_ffn(h, w_up, conv_w, conv_b, w_down):
    T = h.shape[1]
    u = h @ w_up
    up = jnp.pad(u, ((0, 0), (CONV_W - 1, 0), (0, 0)))
    c = conv_b + sum(up[:, j:j + T] * conv_w[j] for j in range(CONV_W))
    gate, val = jnp.split(c, 2, axis=-1)
    return (jax.nn.gelu(gate) * val) @ w_down


def setup_inputs(seed: int = 0) -> dict:
    key = jax.random.key(seed)
    ks = jax.random.split(key, 24)
    f32 = jnp.float32
    nrm = lambda k, shape, s: jax.random.normal(k, shape, f32) * s
    gain = lambda k, shape: 1.0 + 0.01 * jax.random.normal(k, shape, f32)
    x = jax.random.normal(ks[0], (BATCH, SEQ, D_MODEL), f32)
    p = jax.random.normal(ks[1], (DEPTH, BATCH, SEQ, PLE_DIM), f32)
    offsets = jax.random.randint(ks[2], (BATCH, 1), 0, 64, dtype=jnp.int32) * CHUNK
    positions = jnp.arange(SEQ, dtype=jnp.int32)[None, :] + offsets
    return {
        "x": x,
        "p": p,
        "positions": positions,
        "norm_mix": gain(ks[3], (DEPTH, D_MODEL)),
        "w_in": nrm(ks[4], (DEPTH, D_MODEL, IN_COLS), D_MODEL ** -0.5),
        "w_a_up": nrm(ks[5], (DEPTH, GLA_GATE_RANK, GLA_HEADS * GLA_DK), GLA_GATE_RANK ** -0.5),
        "b_a": nrm(ks[6], (DEPTH, GLA_HEADS * GLA_DK), 0.01),
        "gla_norm": gain(ks[7], (DEPTH, GLA_WIDTH)),
        "lam_q1": nrm(ks[8], (DEPTH, DIFF_DQK), 0.1),
        "lam_k1": nrm(ks[9], (DEPTH, DIFF_DQK), 0.1),
        "lam_q2": nrm(ks[10], (DEPTH, DIFF_DQK), 0.1),
        "lam_k2": nrm(ks[11], (DEPTH, DIFF_DQK), 0.1),
        "diff_norm": gain(ks[12], (DEPTH, DIFF_WIDTH)),
        "w_out": nrm(ks[13], (DEPTH, D_MODEL, D_MODEL), D_MODEL ** -0.5),
        "norm_ffn": gain(ks[14], (DEPTH, D_MODEL)),
        "w_up": nrm(ks[15], (DEPTH, D_MODEL, 2 * D_FF), D_MODEL ** -0.5),
        "conv_w": nrm(ks[16], (DEPTH, CONV_W, 2 * D_FF), CONV_W ** -0.5),
        "conv_b": nrm(ks[17], (DEPTH, 2 * D_FF), 0.01),
        "w_down": nrm(ks[18], (DEPTH, D_FF, D_MODEL), D_FF ** -0.5),
        "norm_ple": gain(ks[19], (DEPTH, D_MODEL)),
        "w_ple_gate": nrm(ks[20], (DEPTH, D_MODEL, D_MODEL), D_MODEL ** -0.5),
        "w_ple_proj": nrm(ks[21], (DEPTH, PLE_DIM, D_MODEL), PLE_DIM ** -0.5),
        "norm_final": gain(ks[22], (D_MODEL,)),
    }


def reference(x, p, positions, norm_mix, w_in, w_a_up, b_a, gla_norm, lam_q1, lam_k1, lam_q2, lam_k2,
              diff_norm, w_out, norm_ffn, w_up, conv_w, conv_b, w_down, norm_ple, w_ple_gate,
              w_ple_proj, norm_final):
    split_points = [int(s) for s in np.cumsum(IN_SIZES)[:-1]]
    h = x
    for i in range(DEPTH):
        u = rms_norm(h, norm_mix[i])
        z = u @ w_in[i]
        gq, gk, gv, gg, ga, dq, dk, dv = jnp.split(z, split_points, axis=-1)
        o_gla = gla_mixer(gq, gk, gv, gg, ga, w_a_up[i], b_a[i], gla_norm[i])
        lam_init = 0.8 - 0.6 * math.exp(-0.3 * i)
        lam = (jnp.exp(jnp.sum(lam_q1[i].astype(jnp.float32) * lam_k1[i].astype(jnp.float32)))
               - jnp.exp(jnp.sum(lam_q2[i].astype(jnp.float32) * lam_k2[i].astype(jnp.float32)))
               + lam_init)
        o_diff = diff_attention(dq, dk, dv, positions, lam, lam_init, diff_norm[i])
        h = h + jnp.concatenate([o_gla, o_diff], axis=-1) @ w_out[i]
        h = h + conv_ffn(rms_norm(h, norm_ffn[i]), w_up[i], conv_w[i], conv_b[i], w_down[i])
        gate = jax.nn.sigmoid(rms_norm(h, norm_ple[i]) @ w_ple_gate[i])
        h = h + gate * (p[i] @ w_ple_proj[i])
    return rms_norm(h, norm_final)
```

```python
import functools
import math

import jax
import jax.numpy as jnp
from jax import lax
from jax.experimental import pallas as pl
from jax.experimental.pallas import tpu as pltpu

F32 = jnp.float32
BF16 = jnp.bfloat16

CHUNK = 64
ROPE_THETA = 10000.0
NORM_EPS = 1e-6
GLA_HEADS = 4
GLA_DK = 64
GLA_DV = 128
GLA_GATE_RANK = 16
GLA_GATE_NORM = 16.0
DIFF_HEADS = 4
DIFF_DQK = 64
DIFF_DV = 128
CONV_W = 3
NEG_INF = -1e30

LANES = 128
SUBLANES = 8
VMEM_LIMIT_BYTES = 56 * 1024 * 1024

TM_IN = 512
TC_GLA = 512
TQ_ATT = 256
TK_ATT = 256
TM_OUT = 512
FF_CHUNK = 256


def _rms(x, g):
    return x * lax.rsqrt(jnp.mean(x * x, axis=-1, keepdims=True) + NORM_EPS) * g


def _dot(a, b):
    return jnp.dot(a, b, preferred_element_type=F32)


def _dot_nt(a, b):
    return lax.dot_general(a, b, (((1,), (1,)), ((), ())), preferred_element_type=F32)


def _dot_tn(a, b):
    return lax.dot_general(a, b, (((0,), (0,)), ((), ())), preferred_element_type=F32)


def _inproj_body(x_ref, pos_ref, g_ref, w_ref, wa_ref, ba_ref, invf_ref, sgn_ref,
                 gq_ref, gk_ref, gv_ref, gg_ref, la_ref, dq_ref, dk_ref, dv_ref):
    u = _rms(x_ref[...], g_ref[...]).astype(BF16)

    def proj(c0, c1):
        return _dot(u, w_ref[:, c0:c1])

    gq_ref[...] = (proj(0, 256) * (GLA_DK ** -0.5)).astype(BF16)
    gk_ref[...] = proj(256, 512).astype(BF16)
    gv_ref[...] = proj(512, 1024).astype(BF16)
    gg_ref[...] = proj(1024, 1536).astype(BF16)

    ang = pos_ref[...].astype(F32) * invf_ref[...]
    cos = jnp.cos(ang)
    sin = jnp.sin(ang) * sgn_ref[...]
    lane = lax.broadcasted_iota(jnp.int32, ang.shape, 1)
    first_half = (lane & (DIFF_DQK // 2)) == 0

    def rope(z, scale):
        outs = []
        for j in range(z.shape[1] // LANES):
            zj = z[:, j * LANES:(j + 1) * LANES]
            rot = jnp.where(first_half,
                            pltpu.roll(zj, LANES - DIFF_DQK // 2, 1),
                            pltpu.roll(zj, DIFF_DQK // 2, 1))
            outs.append(((zj * cos + rot * sin) * scale).astype(BF16))
        return jnp.concatenate(outs, axis=1)

    dq_ref[...] = rope(proj(1536, 2048), DIFF_DQK ** -0.5)
    dk_ref[...] = rope(proj(2048, 2560), 1.0)
    dv_ref[...] = proj(2560, 3072).astype(BF16)

    a_low = proj(3072, 3200).astype(BF16)
    pre = _dot(a_low, wa_ref[...]) + ba_ref[...]
    log_sig = jnp.minimum(pre, 0.0) - jnp.log1p(jnp.exp(-jnp.abs(pre)))
    la_ref[...] = log_sig * (1.0 / GLA_GATE_NORM)


def _inproj(x2, pos2, norm_mix, w_in_r, wa_pad, b_a, invf, sgn):
    n, d = x2.shape
    tm = TM_IN
    row = lambda i: (i, 0)
    const = lambda i: (0, 0)
    widths = (256, 256, 512, 512, 256, 512, 512, 512)
    dtypes = (BF16, BF16, BF16, BF16, F32, BF16, BF16, BF16)
    return pl.pallas_call(
        _inproj_body,
        out_shape=[jax.ShapeDtypeStruct((n, w), dt) for w, dt in zip(widths, dtypes)],
        grid=(n // tm,),
        in_specs=[
            pl.BlockSpec((tm, d), row),
            pl.BlockSpec((tm, 1), row),
            pl.BlockSpec((1, d), const),
            pl.BlockSpec(w_in_r.shape, const),
            pl.BlockSpec(wa_pad.shape, const),
            pl.BlockSpec((1, 256), const),
            pl.BlockSpec((1, LANES), const),
            pl.BlockSpec((1, LANES), const),
        ],
        out_specs=[pl.BlockSpec((tm, w), row) for w in widths],
        compiler_params=pltpu.CompilerParams(
            dimension_semantics=("arbitrary",), vmem_limit_bytes=VMEM_LIMIT_BYTES),
        name="inproj",
    )(x2, pos2, norm_mix, w_in_r, wa_pad, b_a, invf, sgn)


def _gla_body(q_ref, k_ref, v_ref, g_ref, la_ref, gn_ref, o_ref, st_ref):
    @pl.when(pl.program_id(1) == 0)
    def _():
        st_ref[...] = jnp.zeros_like(st_ref)

    hk = GLA_HEADS * GLA_DK
    r_i = lax.broadcasted_iota(jnp.int32, (CHUNK, CHUNK), 0)
    c_i = lax.broadcasted_iota(jnp.int32, (CHUNK, CHUNK), 1)
    tril = (r_i >= c_i).astype(BF16)
    lane_head = lax.broadcasted_iota(jnp.int32, (1, hk), 1) // GLA_DK
    head_mask = [lane_head == h for h in range(GLA_HEADS)]
    rr = lax.broadcasted_iota(jnp.int32, (GLA_HEADS * CHUNK, CHUNK), 0) % CHUNK
    cc = lax.broadcasted_iota(jnp.int32, (GLA_HEADS * CHUNK, CHUNK), 1)
    tri_stack = rr >= cc
    gn = gn_ref[...]

    def chunk(c, carry):
        r0 = pl.multiple_of(c * CHUNK, CHUNK)
        rows = pl.ds(r0, CHUNK)
        la = la_ref[rows, :]
        la_hi = la.astype(BF16)
        la_lo = (la - la_hi.astype(F32)).astype(BF16)
        b = _dot(tril, la_hi) + _dot(tril, la_lo)
        b_last = b[CHUNK - 1:CHUNK, :]
        eb = jnp.exp(b)
        enb = jnp.exp(-b)
        q = q_ref[rows, :].astype(F32)
        k = k_ref[rows, :].astype(F32)
        qf = q * eb
        qb = q * enb
        k_enb = (k * enb).astype(BF16)
        k_eb = (k * eb).astype(BF16)
        k_dec = (k * jnp.exp(b_last - b)).astype(BF16)
        decay = jnp.exp(b_last)

        qf_st = jnp.concatenate([jnp.where(m, qf, 0.0) for m in head_mask], axis=0).astype(BF16)
        qb_st = jnp.concatenate([jnp.where(m, qb, 0.0) for m in head_mask], axis=0).astype(BF16)
        a_fwd = _dot_nt(qf_st, k_enb)
        a_bwd = _dot_nt(qb_st, k_eb)
        a = jnp.where(tri_stack, a_fwd, a_bwd).astype(BF16)

        st = st_ref[...]
        o_inter = _dot_nt(qf_st, st.astype(BF16))
        v = v_ref[rows, :]
        for h in range(GLA_HEADS):
            hs = slice(h * CHUNK, (h + 1) * CHUNK)
            vs = slice(h * GLA_DV, (h + 1) * GLA_DV)
            o = _dot(a[hs, :], v[:, vs]) + o_inter[hs, :]
            o = _rms(o, gn[:, vs])
            g = g_ref[rows, vs].astype(F32)
            o_ref[rows, vs] = (o * (g * jax.nn.sigmoid(g))).astype(BF16)

        pt = _dot_tn(v, k_dec)
        d_st = jnp.where(head_mask[0], pt[0:GLA_DV, :], 0.0)
        for h in range(1, GLA_HEADS):
            d_st = d_st + jnp.where(head_mask[h], pt[h * GLA_DV:(h + 1) * GLA_DV, :], 0.0)
        st_ref[...] = st * decay + d_st
        return carry

    lax.fori_loop(0, q_ref.shape[0] // CHUNK, chunk, 0)


def _gla(gq, gk, gv, gg, la, gla_norm, batch, seq):
    tc = TC_GLA
    nt = seq // tc
    row = lambda b, t: (b * nt + t, 0)
    const = lambda b, t: (0, 0)
    hk = GLA_HEADS * GLA_DK
    hv = GLA_HEADS * GLA_DV
    return pl.pallas_call(
        _gla_body,
        out_shape=jax.ShapeDtypeStruct((batch * seq, hv), BF16),
        grid=(batch, nt),
        in_specs=[
            pl.BlockSpec((tc, hk), row),
            pl.BlockSpec((tc, hk), row),
            pl.BlockSpec((tc, hv), row),
            pl.BlockSpec((tc, hv), row),
            pl.BlockSpec((tc, hk), row),
            pl.BlockSpec((1, hv), const),
        ],
        out_specs=pl.BlockSpec((tc, hv), row),
        scratch_shapes=[pltpu.VMEM((GLA_DV, hk), F32)],
        compiler_params=pltpu.CompilerParams(
            dimension_semantics=("arbitrary", "arbitrary"), vmem_limit_bytes=VMEM_LIMIT_BYTES),
        name="gla",
    )(gq, gk, gv, gg, la, gla_norm)


def _diff_body(lam_init, lq1_ref, lk1_ref, lq2_ref, lk2_ref, q_ref, k_ref, v_ref, gn_ref,
               o_ref, m_ref, l_ref, acc_ref):
    tq = q_ref.shape[0]
    tk = TK_ATT
    qi = pl.program_id(2)

    q = q_ref[...]
    lane = lax.broadcasted_iota(jnp.int32, q.shape, 1)
    zero = jnp.zeros_like(q)
    qs = jnp.concatenate([jnp.where(lane < DIFF_DQK, q, zero),
                          jnp.where(lane >= DIFF_DQK, q, zero)], axis=0)

    m_ref[...] = jnp.full_like(m_ref, NEG_INF)
    l_ref[...] = jnp.zeros_like(l_ref)
    acc_ref[...] = jnp.zeros_like(acc_ref)

    def step(j, masked):
        k0 = pl.multiple_of(j * tk, tk)
        kb = k_ref[pl.ds(k0, tk), :]
        vb = v_ref[pl.ds(k0, tk), :]
        s = _dot_nt(qs, kb)
        if masked:
            q_chunk = (lax.broadcasted_iota(jnp.int32, s.shape, 0) % tq) // CHUNK
            k_chunk = lax.broadcasted_iota(jnp.int32, s.shape, 1) // CHUNK
            s = jnp.where(k_chunk <= q_chunk, s, NEG_INF)
        m_prev = m_ref[...]
        m_next = jnp.maximum(m_prev, jnp.max(s, axis=1, keepdims=True))
        alpha = jnp.exp(m_prev - m_next)
        p = jnp.exp(s - jnp.concatenate([m_next] * (tk // LANES), axis=1))
        l_ref[...] = alpha * l_ref[...] + jnp.sum(p, axis=1, keepdims=True)
        acc_ref[...] = alpha * acc_ref[...] + _dot(p.astype(BF16), vb)
        m_ref[...] = m_next

    n_full = (qi * tq) // tk

    def body(j, carry):
        step(j, False)
        return carry

    lax.fori_loop(0, n_full, body, 0)
    for jd in range(tq // tk):
        step(n_full + jd, True)

    lam = (jnp.exp(jnp.sum(lq1_ref[...] * lk1_ref[...], axis=1, keepdims=True))
           - jnp.exp(jnp.sum(lq2_ref[...] * lk2_ref[...], axis=1, keepdims=True))
           + lam_init)
    o_maps = acc_ref[...] / l_ref[...]
    o = o_maps[:tq, :] - lam * o_maps[tq:, :]
    o = _rms(o, gn_ref[...]) * (1.0 - lam_init)
    o_ref[...] = o.astype(BF16)


def _diff_attn(dq, dk, dv, lam_vecs, diff_norm, lam_init, batch, seq):
    tq = TQ_ATT
    nq = seq // tq
    hv = DIFF_HEADS * DIFF_DV
    vec = pl.BlockSpec((1, DIFF_DQK), lambda b, h, i: (0, 0))
    return pl.pallas_call(
        functools.partial(_diff_body, lam_init),
        out_shape=jax.ShapeDtypeStruct((batch * seq, hv), BF16),
        grid=(batch, DIFF_HEADS, nq),
        in_specs=[
            vec, vec, vec, vec,
            pl.BlockSpec((tq, LANES), lambda b, h, i: (b * nq + i, h)),
            pl.BlockSpec((seq, LANES), lambda b, h, i: (b, h)),
            pl.BlockSpec((seq, LANES), lambda b, h, i: (b, h)),
            pl.BlockSpec((1, DIFF_DV), lambda b, h, i: (0, h)),
        ],
        out_specs=pl.BlockSpec((tq, DIFF_DV), lambda b, h, i: (b * nq + i, h)),
        scratch_shapes=[
            pltpu.VMEM((2 * tq, LANES), F32),
            pltpu.VMEM((2 * tq, LANES), F32),
            pltpu.VMEM((2 * tq, DIFF_DV), F32),
        ],
        compiler_params=pltpu.CompilerParams(
            dimension_semantics=("arbitrary", "arbitrary", "arbitrary"),
            vmem_limit_bytes=VMEM_LIMIT_BYTES),
        name="diff_attn",
    )(*lam_vecs, dq, dk, dv, diff_norm)


def _tail_body(tiles_per_seq, final, x_ref, og_ref, od_ref, p_ref, wo_ref, nf_ref, wg_ref, wv_ref,
               cw_ref, cb_ref, wd_ref, npl_ref, wpg_ref, wpp_ref, nfin_ref,
               y_ref, carry_ref, acc_ref):
    tm = x_ref.shape[0]
    n_chunks = wg_ref.shape[0]

    @pl.when(pl.program_id(0) % tiles_per_seq == 0)
    def _():
        carry_ref[...] = jnp.zeros_like(carry_ref)

    half = og_ref.shape[1]
    h1 = (x_ref[...] + _dot(og_ref[...], wo_ref[0:half, :])
          + _dot(od_ref[...], wo_ref[half:2 * half, :]))
    u = _rms(h1, nf_ref[...]).astype(BF16)

    row8 = lax.broadcasted_iota(jnp.int32, (SUBLANES, FF_CHUNK), 0)

    def shifted(z, hist):
        r1 = pltpu.roll(z, 1, 0)
        r2 = pltpu.roll(z, 2, 0)
        h_m1 = hist[SUBLANES - 1:SUBLANES, :]
        h_m2 = hist[SUBLANES - 2:SUBLANES - 1, :]
        top1 = jnp.where(row8 == 0, h_m1, r1[:SUBLANES, :])
        top2 = jnp.where(row8 == 0, h_m2, jnp.where(row8 == 1, h_m1, r2[:SUBLANES, :]))
        s1 = jnp.concatenate([top1, r1[SUBLANES:, :]], axis=0)
        s2 = jnp.concatenate([top2, r2[SUBLANES:, :]], axis=0)
        return s1, s2

    def conv(z, hist, w, b):
        s1, s2 = shifted(z, hist)
        return b + s2 * w[0:1, :] + s1 * w[1:2, :] + z * w[2:3, :]

    acc_ref[...] = jnp.zeros_like(acc_ref)

    def ff_chunk(j, carry):
        zg = _dot(u, wg_ref[j])
        zv = _dot(u, wv_ref[j])
        hist = carry_ref[j]
        cw = cw_ref[j]
        cb = cb_ref[j]
        c = FF_CHUNK
        gate = conv(zg, hist[:, :c], cw[:, :c], cb[:, :c])
        val = conv(zv, hist[:, c:], cw[:, c:], cb[:, c:])
        carry_ref[j] = jnp.concatenate([zg[tm - SUBLANES:, :], zv[tm - SUBLANES:, :]], axis=1)
        act = (jax.nn.gelu(gate) * val).astype(BF16)
        acc_ref[...] += _dot(act, wd_ref[j])
        return carry

    lax.fori_loop(0, n_chunks, ff_chunk, 0)

    h2 = h1 + acc_ref[...]
    gate = jax.nn.sigmoid(_dot(_rms(h2, npl_ref[...]).astype(BF16), wpg_ref[...]))
    h3 = h2 + gate * _dot(p_ref[...].astype(BF16), wpp_ref[...])
    y_ref[...] = _rms(h3, nfin_ref[...]) if final else h3


def _tail(x2, o_gla, o_diff, p2, w_out, norm_ffn, wg, wv, cw, cb, wd, norm_ple, w_pg, w_pp,
          norm_final, seq, final):
    n, d = x2.shape
    tm = TM_OUT
    row = lambda i: (i, 0)
    c2 = lambda i: (0, 0)
    c3 = lambda i: (0, 0, 0)
    one = pl.Buffered(1)

    def resident(a):
        return pl.BlockSpec(a.shape, c3 if a.ndim == 3 else c2, pipeline_mode=one)

    n_chunks = wg.shape[0]
    return pl.pallas_call(
        functools.partial(_tail_body, seq // tm, final),
        out_shape=jax.ShapeDtypeStruct((n, d), F32),
        grid=(n // tm,),
        in_specs=[
            pl.BlockSpec((tm, d), row),
            pl.BlockSpec((tm, o_gla.shape[1]), row),
            pl.BlockSpec((tm, o_diff.shape[1]), row),
            pl.BlockSpec((tm, p2.shape[1]), row),
            resident(w_out), resident(norm_ffn), resident(wg), resident(wv), resident(cw),
            resident(cb), resident(wd), resident(norm_ple), resident(w_pg), resident(w_pp),
            resident(norm_final),
        ],
        out_specs=pl.BlockSpec((tm, d), row),
        scratch_shapes=[
            pltpu.VMEM((n_chunks, SUBLANES, 2 * FF_CHUNK), F32),
            pltpu.VMEM((tm, d), F32),
        ],
        compiler_params=pltpu.CompilerParams(
            dimension_semantics=("arbitrary",), vmem_limit_bytes=VMEM_LIMIT_BYTES),
        name="tail",
    )(x2, o_gla, o_diff, p2, w_out, norm_ffn, wg, wv, cw, cb, wd, norm_ple, w_pg, w_pp,
      norm_final)


def kernel(x, p, positions, norm_mix, w_in, w_a_up, b_a, gla_norm, lam_q1, lam_k1, lam_q2, lam_k2,
           diff_norm, w_out, norm_ffn, w_up, conv_w, conv_b, w_down, norm_ple, w_ple_gate,
           w_ple_proj, norm_final):
    batch, seq, d = x.shape
    depth = w_in.shape[0]
    n = batch * seq
    d_ff = w_down.shape[1]
    n_chunks = d_ff // FF_CHUNK
    assert seq % max(TM_IN, TC_GLA, TQ_ATT, TM_OUT) == 0 and TQ_ATT % TK_ATT == 0
    assert d_ff % FF_CHUNK == 0 and TK_ATT % CHUNK == 0

    inv_freq = ROPE_THETA ** (-jnp.arange(0, DIFF_DQK, 2, dtype=F32) / DIFF_DQK)
    invf = jnp.tile(inv_freq, LANES // (DIFF_DQK // 2))[None, :]
    sgn = jnp.tile(jnp.concatenate([-jnp.ones((DIFF_DQK // 2,), F32),
                                    jnp.ones((DIFF_DQK // 2,), F32)]), LANES // DIFF_DQK)[None, :]

    h = x.reshape(n, d)
    pos2 = positions.reshape(n, 1)
    for i in range(depth):
        wi = w_in[i]
        ga0 = 1536
        w_in_r = jnp.concatenate(
            [wi[:, :ga0], wi[:, ga0 + GLA_GATE_RANK:], wi[:, ga0:ga0 + GLA_GATE_RANK],
             jnp.zeros((d, LANES - GLA_GATE_RANK), F32)], axis=1).astype(BF16)
        wa_pad = jnp.concatenate(
            [w_a_up[i], jnp.zeros((LANES - GLA_GATE_RANK, w_a_up.shape[2]), F32)], axis=0).astype(BF16)

        gq, gk, gv, gg, la, dq, dk, dv = _inproj(
            h, pos2, norm_mix[i][None, :], w_in_r, wa_pad, b_a[i][None, :], invf, sgn)
        o_gla = _gla(gq, gk, gv, gg, la, gla_norm[i][None, :], batch, seq)

        lam_init = 0.8 - 0.6 * math.exp(-0.3 * i)
        lam_vecs = [v[i][None, :] for v in (lam_q1, lam_k1, lam_q2, lam_k2)]
        o_diff = _diff_attn(dq, dk, dv, lam_vecs, diff_norm[i][None, :], lam_init, batch, seq)

        wu = w_up[i].astype(BF16)
        wg = wu[:, :d_ff].reshape(d, n_chunks, FF_CHUNK).transpose(1, 0, 2)
        wv = wu[:, d_ff:].reshape(d, n_chunks, FF_CHUNK).transpose(1, 0, 2)
        wd = w_down[i].astype(BF16).reshape(n_chunks, FF_CHUNK, d)

        def chunked(a):
            r = a.shape[0]
            g = a[:, :d_ff].reshape(r, n_chunks, FF_CHUNK)
            v = a[:, d_ff:].reshape(r, n_chunks, FF_CHUNK)
            return jnp.concatenate([g, v], axis=2).transpose(1, 0, 2)

        cw = chunked(jnp.concatenate(
            [conv_w[i], jnp.zeros((SUBLANES - CONV_W, 2 * d_ff), F32)], axis=0))
        cb = chunked(conv_b[i][None, :])
        h = _tail(h, o_gla, o_diff, p[i].reshape(n, -1), w_out[i].astype(BF16),
                  norm_ffn[i][None, :], wg, wv, cw, cb, wd, norm_ple[i][None, :],
                  w_ple_gate[i].astype(BF16), w_ple_proj[i].astype(BF16),
                  norm_final[None, :], seq, i == depth - 1)
    return h.reshape(batch, seq, d)
```

```python
import functools
import math

import jax
import jax.numpy as jnp
from jax import lax
from jax.experimental import pallas as pl
from jax.experimental.pallas import tpu as pltpu

F32 = jnp.float32
BF16 = jnp.bfloat16

CHUNK = 64
ROPE_THETA = 10000.0
NORM_EPS = 1e-6
GLA_HEADS = 4
GLA_DK = 64
GLA_DV = 128
GLA_GATE_RANK = 16
GLA_GATE_NORM = 16.0
DIFF_HEADS = 4
DIFF_DQK = 64
DIFF_DV = 128
CONV_W = 3
NEG_INF = -1e30
LOG2E = 1.4426950408889634
GELU_A = 2.0 * math.sqrt(2.0 / math.pi)
GELU_B = GELU_A * 0.044715

LANES = 128
SUBLANES = 8
VMEM_LIMIT_BYTES = 56 * 1024 * 1024

TM_IN = 512
TC_GLA = 512
TQ_ATT = 256
TK_ATT = 256
HB_ATT = 4
TM_OUT = 512
FF_CHUNK = 256


def _rms(x, g):
    return x * lax.rsqrt(jnp.mean(x * x, axis=-1, keepdims=True) + NORM_EPS) * g


def _dot(a, b):
    return jnp.dot(a, b, preferred_element_type=F32)


def _dot_nt(a, b):
    return lax.dot_general(a, b, (((1,), (1,)), ((), ())), preferred_element_type=F32)


def _dot_tn(a, b):
    return lax.dot_general(a, b, (((0,), (0,)), ((), ())), preferred_element_type=F32)


def _inproj_body(x_ref, pos_ref, g_ref, w_ref, wvt_ref, wa_ref, ba_ref, invf_ref, sgn_ref,
                 gq_ref, gk_ref, gv_ref, gg_ref, la_ref, dq_ref, dk_ref, dvt_ref):
    u = _rms(x_ref[...], g_ref[...]).astype(BF16)

    def proj(c0, c1):
        return _dot(u, w_ref[:, c0:c1])

    gq_ref[...] = (proj(0, 256) * (GLA_DK ** -0.5)).astype(BF16)
    gk_ref[...] = proj(256, 512).astype(BF16)
    gv_ref[...] = proj(512, 1024).astype(BF16)
    gg_ref[...] = proj(1024, 1536).astype(BF16)

    ang = pos_ref[...].astype(F32) * invf_ref[...]
    cos = jnp.cos(ang)
    sin = jnp.sin(ang) * sgn_ref[...]
    lane = lax.broadcasted_iota(jnp.int32, ang.shape, 1)
    first_half = (lane & (DIFF_DQK // 2)) == 0

    def rope(z, scale):
        outs = []
        for j in range(z.shape[1] // LANES):
            zj = z[:, j * LANES:(j + 1) * LANES]
            rot = jnp.where(first_half,
                            pltpu.roll(zj, LANES - DIFF_DQK // 2, 1),
                            pltpu.roll(zj, DIFF_DQK // 2, 1))
            outs.append(((zj * cos + rot * sin) * scale).astype(BF16))
        return jnp.concatenate(outs, axis=1)

    dq_ref[...] = rope(proj(1536, 2048), DIFF_DQK ** -0.5 * LOG2E)
    dk_ref[...] = rope(proj(2048, 2560), 1.0)
    dvt = _dot_nt(wvt_ref[...], u)
    tk = dvt_ref.shape[2]
    for c in range(dvt_ref.shape[0]):
        dvt_ref[c] = dvt[:, c * tk:(c + 1) * tk].astype(BF16)

    a_low = proj(2560, 2688).astype(BF16)
    pre = _dot(a_low, wa_ref[...]) + ba_ref[...]
    log_sig = jnp.minimum(pre, 0.0) - jnp.log1p(jnp.exp(-jnp.abs(pre)))
    la_ref[...] = log_sig * (1.0 / GLA_GATE_NORM)


def _inproj(x2, pos2, norm_mix, w_in_r, w_dvt, wa_pad, b_a, invf, sgn):
    n, d = x2.shape
    tm = TM_IN
    row = lambda i: (i, 0)
    const = lambda i: (0, 0)
    widths = (256, 256, 512, 512, 256, 512, 512)
    dtypes = (BF16, BF16, BF16, BF16, F32, BF16, BF16)
    tk = TK_ATT
    hv = w_dvt.shape[0]
    return pl.pallas_call(
        _inproj_body,
        out_shape=[jax.ShapeDtypeStruct((n, w), dt) for w, dt in zip(widths, dtypes)]
        + [jax.ShapeDtypeStruct((n // tk, hv, tk), BF16)],
        grid=(n // tm,),
        in_specs=[
            pl.BlockSpec((tm, d), row),
            pl.BlockSpec((tm, 1), row),
            pl.BlockSpec((1, d), const),
            pl.BlockSpec(w_in_r.shape, const),
            pl.BlockSpec(w_dvt.shape, const),
            pl.BlockSpec(wa_pad.shape, const),
            pl.BlockSpec((1, 256), const),
            pl.BlockSpec((1, LANES), const),
            pl.BlockSpec((1, LANES), const),
        ],
        out_specs=[pl.BlockSpec((tm, w), row) for w in widths]
        + [pl.BlockSpec((tm // tk, hv, tk), lambda i: (i, 0, 0))],
        compiler_params=pltpu.CompilerParams(
            dimension_semantics=("arbitrary",), vmem_limit_bytes=VMEM_LIMIT_BYTES),
        name="inproj",
    )(x2, pos2, norm_mix, w_in_r, w_dvt, wa_pad, b_a, invf, sgn)


def _gla_body(q_ref, k_ref, v_ref, g_ref, la_ref, gn_ref, o_ref, st_ref):
    @pl.when(pl.program_id(1) == 0)
    def _():
        st_ref[...] = jnp.zeros_like(st_ref)

    hk = GLA_HEADS * GLA_DK
    r_i = lax.broadcasted_iota(jnp.int32, (CHUNK, CHUNK), 0)
    c_i = lax.broadcasted_iota(jnp.int32, (CHUNK, CHUNK), 1)
    tril = (r_i >= c_i).astype(BF16)
    lane_head = lax.broadcasted_iota(jnp.int32, (1, hk), 1) // GLA_DK
    head_mask = [lane_head == h for h in range(GLA_HEADS)]
    rr = lax.broadcasted_iota(jnp.int32, (GLA_HEADS * CHUNK, CHUNK), 0) % CHUNK
    cc = lax.broadcasted_iota(jnp.int32, (GLA_HEADS * CHUNK, CHUNK), 1)
    tri_stack = rr >= cc
    gn = gn_ref[...]

    def chunk(c, carry):
        r0 = pl.multiple_of(c * CHUNK, CHUNK)
        rows = pl.ds(r0, CHUNK)
        la = la_ref[rows, :]
        la_hi = la.astype(BF16)
        la_lo = (la - la_hi.astype(F32)).astype(BF16)
        b = _dot(tril, la_hi) + _dot(tril, la_lo)
        b_last = b[CHUNK - 1:CHUNK, :]
        eb = jnp.exp(b)
        enb = jnp.exp(-b)
        q = q_ref[rows, :].astype(F32)
        k = k_ref[rows, :].astype(F32)
        qf = q * eb
        qb = q * enb
        k_enb = (k * enb).astype(BF16)
        k_eb = (k * eb).astype(BF16)
        k_dec = (k * jnp.exp(b_last - b)).astype(BF16)
        decay = jnp.exp(b_last)

        qf_st = jnp.concatenate([jnp.where(m, qf, 0.0) for m in head_mask], axis=0).astype(BF16)
        qb_st = jnp.concatenate([jnp.where(m, qb, 0.0) for m in head_mask], axis=0).astype(BF16)
        a_fwd = _dot_nt(qf_st, k_enb)
        a_bwd = _dot_nt(qb_st, k_eb)
        a = jnp.where(tri_stack, a_fwd, a_bwd).astype(BF16)

        st = st_ref[...]
        o_inter = _dot_nt(qf_st, st.astype(BF16))
        v = v_ref[rows, :]
        for h in range(GLA_HEADS):
            hs = slice(h * CHUNK, (h + 1) * CHUNK)
            vs = slice(h * GLA_DV, (h + 1) * GLA_DV)
            o = _dot(a[hs, :], v[:, vs]) + o_inter[hs, :]
            o = _rms(o, gn[:, vs])
            g = g_ref[rows, vs].astype(F32)
            o_ref[rows, vs] = (o * (g * jax.nn.sigmoid(g))).astype(BF16)

        pt = _dot_tn(v, k_dec)
        d_st = jnp.where(head_mask[0], pt[0:GLA_DV, :], 0.0)
        for h in range(1, GLA_HEADS):
            d_st = d_st + jnp.where(head_mask[h], pt[h * GLA_DV:(h + 1) * GLA_DV, :], 0.0)
        st_ref[...] = st * decay + d_st
        return carry

    lax.fori_loop(0, q_ref.shape[0] // CHUNK, chunk, 0)


def _gla(gq, gk, gv, gg, la, gla_norm, batch, seq):
    tc = TC_GLA
    nt = seq // tc
    row = lambda b, t: (b * nt + t, 0)
    const = lambda b, t: (0, 0)
    hk = GLA_HEADS * GLA_DK
    hv = GLA_HEADS * GLA_DV
    return pl.pallas_call(
        _gla_body,
        out_shape=jax.ShapeDtypeStruct((batch * seq, hv), BF16),
        grid=(batch, nt),
        in_specs=[
            pl.BlockSpec((tc, hk), row),
            pl.BlockSpec((tc, hk), row),
            pl.BlockSpec((tc, hv), row),
            pl.BlockSpec((tc, hv), row),
            pl.BlockSpec((tc, hk), row),
            pl.BlockSpec((1, hv), const),
        ],
        out_specs=pl.BlockSpec((tc, hv), row),
        scratch_shapes=[pltpu.VMEM((GLA_DV, hk), F32)],
        compiler_params=pltpu.CompilerParams(
            dimension_semantics=("arbitrary", "arbitrary"), vmem_limit_bytes=VMEM_LIMIT_BYTES),
        name="gla",
    )(gq, gk, gv, gg, la, gla_norm)


def _diff_body(lam_init, lq1_ref, lk1_ref, lq2_ref, lk2_ref, q_ref, k_ref, vt_ref, gn_ref,
               o_ref, m_ref, l_ref, acc_ref, s_ref, mb_ref):
    tq = q_ref.shape[0]
    tk = vt_ref.shape[2]
    hb = q_ref.shape[1] // LANES
    qi = pl.program_id(2)

    lane = lax.broadcasted_iota(jnp.int32, (tq, LANES), 1)
    qs = []
    for a in range(hb):
        q = q_ref[:, a * LANES:(a + 1) * LANES]
        zero = jnp.zeros_like(q)
        qs.append(jnp.where(lane < DIFF_DQK, q, zero))
        qs.append(jnp.where(lane >= DIFF_DQK, q, zero))
    nc = 2 * hb

    heads = [slice((c // 2) * LANES, (c // 2 + 1) * LANES) for c in range(nc)]

    m_ref[...] = jnp.full_like(m_ref, NEG_INF)
    l_ref[...] = jnp.zeros_like(l_ref)
    acc_ref[...] = jnp.zeros_like(acc_ref)

    def scores(t, masked):
        k0 = pl.multiple_of(t * tk, tk)
        s_new = [_dot_nt(k_ref[pl.ds(k0, tk), heads[c]], qs[c]) for c in range(nc)]
        if masked:
            k_chunk = t * (tk // CHUNK) + lax.broadcasted_iota(jnp.int32, (tk, tq), 0) // CHUNK
            q_chunk = qi * (tq // CHUNK) + lax.broadcasted_iota(jnp.int32, (tk, tq), 1) // CHUNK
            allowed = k_chunk <= q_chunk
            s_new = [jnp.where(allowed, s, NEG_INF) for s in s_new]
        return s_new

    def keep_scores(s_new):
        for c in range(nc):
            s_ref[c] = s_new[c]
            mb_ref[c] = jnp.max(s_new[c], axis=0, keepdims=True)

    def softmax_pv(t):
        for c in range(nc):
            m_prev = m_ref[c]
            m_next = jnp.maximum(m_prev, mb_ref[c])
            alpha = jnp.exp2(m_prev - m_next)
            p = jnp.exp2(s_ref[c] - m_next)
            l_ref[c] = alpha * l_ref[c] + jnp.sum(p, axis=0, keepdims=True)
            m_ref[c] = m_next
            acc_ref[c] = alpha * acc_ref[c] + _dot(vt_ref[t, heads[c], :], p.astype(BF16))

    def both(t, masked):
        s_new = scores(t, masked)
        softmax_pv(t - 1)
        keep_scores(s_new)

    keep_scores(scores(0, True))

    @pl.when(qi > 0)
    def _():
        def body(t, carry):
            both(t, False)
            return carry
        lax.fori_loop(1, qi, body, 0)
        both(qi, True)

    softmax_pv(qi)

    lam = (jnp.exp(jnp.sum(lq1_ref[...] * lk1_ref[...], axis=1, keepdims=True))
           - jnp.exp(jnp.sum(lq2_ref[...] * lk2_ref[...], axis=1, keepdims=True))
           + lam_init)
    for a in range(hb):
        hs = slice(a * LANES, (a + 1) * LANES)
        o0 = acc_ref[2 * a] * (1.0 / l_ref[2 * a])
        o1 = acc_ref[2 * a + 1] * (1.0 / l_ref[2 * a + 1])
        o = (o0 - lam * o1).T
        o = _rms(o, gn_ref[:, hs]) * (1.0 - lam_init)
        o_ref[:, hs] = o.astype(BF16)


def _diff_attn(dq, dk, dvt, lam_vecs, diff_norm, lam_init, batch, seq):
    tq = TQ_ATT
    tk = dvt.shape[2]
    assert tq == tk
    nq = seq // tq
    hb = HB_ATT
    hv = DIFF_HEADS * DIFF_DV
    vec = pl.BlockSpec((1, DIFF_DQK), lambda b, h, i: (0, 0))
    return pl.pallas_call(
        functools.partial(_diff_body, lam_init),
        out_shape=jax.ShapeDtypeStruct((batch * seq, hv), BF16),
        grid=(batch, DIFF_HEADS // hb, nq),
        in_specs=[
            vec, vec, vec, vec,
            pl.BlockSpec((tq, hb * LANES), lambda b, h, i: (b * nq + i, h)),
            pl.BlockSpec((seq, hb * LANES), lambda b, h, i: (b, h)),
            pl.BlockSpec((seq // tk, hb * DIFF_DV, tk), lambda b, h, i: (b, h, 0)),
            pl.BlockSpec((1, hb * DIFF_DV), lambda b, h, i: (0, h)),
        ],
        out_specs=pl.BlockSpec((tq, hb * DIFF_DV), lambda b, h, i: (b * nq + i, h)),
        scratch_shapes=[
            pltpu.VMEM((2 * hb, 1, tq), F32),
            pltpu.VMEM((2 * hb, 1, tq), F32),
            pltpu.VMEM((2 * hb, DIFF_DV, tq), F32),
            pltpu.VMEM((2 * hb, tk, tq), F32),
            pltpu.VMEM((2 * hb, 1, tq), F32),
        ],
        compiler_params=pltpu.CompilerParams(
            dimension_semantics=("arbitrary", "arbitrary", "arbitrary"),
            vmem_limit_bytes=VMEM_LIMIT_BYTES),
        name="diff_attn",
    )(*lam_vecs, dq, dk, dvt, diff_norm)


def _tail_body(tiles_per_seq, final, x_ref, og_ref, od_ref, p_ref, wo_ref, nf_ref, wg_ref, wv_ref,
               cw_ref, cb_ref, wd_ref, npl_ref, wpg_ref, wpp_ref, nfin_ref,
               y_ref, carry_ref, zb_ref, act_ref):
    tm = x_ref.shape[0]
    n_chunks = wg_ref.shape[0]
    c = FF_CHUNK

    @pl.when(pl.program_id(0) % tiles_per_seq == 0)
    def _():
        carry_ref[...] = jnp.zeros_like(carry_ref)

    half = og_ref.shape[1]
    h1 = (x_ref[...] + _dot(og_ref[...], wo_ref[0:half, :])
          + _dot(od_ref[...], wo_ref[half:2 * half, :]))
    u = _rms(h1, nf_ref[...]).astype(BF16)

    def up(j):
        zb = zb_ref.at[j % 2]
        zb[0:SUBLANES, :] = carry_ref[j]
        zb[SUBLANES:, 0:c] = _dot(u, wg_ref[j])
        zb[SUBLANES:, c:] = _dot(u, wv_ref[j])
        carry_ref[j] = zb[tm:tm + SUBLANES, :]

    up(0)
    for j in range(n_chunks):
        if j + 1 < n_chunks:
            up(j + 1)
        zb = zb_ref.at[j % 2]
        cw = cw_ref[j]
        cv = (cb_ref[j] + zb[SUBLANES - 2:SUBLANES - 2 + tm, :] * cw[0:1, :]
              + zb[SUBLANES - 1:SUBLANES - 1 + tm, :] * cw[1:2, :]
              + zb[SUBLANES:, :] * cw[2:3, :])
        gate = cv[:, :c]
        y2 = gate * (GELU_A + GELU_B * (gate * gate))
        act_ref[:, j * c:(j + 1) * c] = (gate / (1.0 + jnp.exp(-y2)) * cv[:, c:]).astype(BF16)

    h2 = h1 + _dot(act_ref[...], wd_ref[...])
    gate = jax.nn.sigmoid(_dot(_rms(h2, npl_ref[...]).astype(BF16), wpg_ref[...]))
    h3 = h2 + gate * _dot(p_ref[...].astype(BF16), wpp_ref[...])
    y_ref[...] = _rms(h3, nfin_ref[...]) if final else h3


def _tail(x2, o_gla, o_diff, p2, w_out, norm_ffn, wg, wv, cw, cb, wd, norm_ple, w_pg, w_pp,
          norm_final, seq, final):
    n, d = x2.shape
    tm = TM_OUT
    row = lambda i: (i, 0)
    c2 = lambda i: (0, 0)
    c3 = lambda i: (0, 0, 0)
    one = pl.Buffered(1)

    def resident(a):
        return pl.BlockSpec(a.shape, c3 if a.ndim == 3 else c2, pipeline_mode=one)

    n_chunks = wg.shape[0]
    return pl.pallas_call(
        functools.partial(_tail_body, seq // tm, final),
        out_shape=jax.ShapeDtypeStruct((n, d), F32),
        grid=(n // tm,),
        in_specs=[
            pl.BlockSpec((tm, d), row),
            pl.BlockSpec((tm, o_gla.shape[1]), row),
            pl.BlockSpec((tm, o_diff.shape[1]), row),
            pl.BlockSpec((tm, p2.shape[1]), row),
            resident(w_out), resident(norm_ffn), resident(wg), resident(wv), resident(cw),
            resident(cb), resident(wd), resident(norm_ple), resident(w_pg), resident(w_pp),
            resident(norm_final),
        ],
        out_specs=pl.BlockSpec((tm, d), row),
        scratch_shapes=[
            pltpu.VMEM((n_chunks, SUBLANES, 2 * FF_CHUNK), F32),
            pltpu.VMEM((2, tm + SUBLANES, 2 * FF_CHUNK), F32),
            pltpu.VMEM((tm, n_chunks * FF_CHUNK), BF16),
        ],
        compiler_params=pltpu.CompilerParams(
            dimension_semantics=("arbitrary",), vmem_limit_bytes=VMEM_LIMIT_BYTES),
        name="tail",
    )(x2, o_gla, o_diff, p2, w_out, norm_ffn, wg, wv, cw, cb, wd, norm_ple, w_pg, w_pp,
      norm_final)


def kernel(x, p, positions, norm_mix, w_in, w_a_up, b_a, gla_norm, lam_q1, lam_k1, lam_q2, lam_k2,
           diff_norm, w_out, norm_ffn, w_up, conv_w, conv_b, w_down, norm_ple, w_ple_gate,
           w_ple_proj, norm_final):
    batch, seq, d = x.shape
    depth = w_in.shape[0]
    n = batch * seq
    d_ff = w_down.shape[1]
    n_chunks = d_ff // FF_CHUNK
    assert seq % max(TM_IN, TC_GLA, TQ_ATT, TM_OUT) == 0 and TM_IN % TK_ATT == 0
    assert d_ff % FF_CHUNK == 0 and TK_ATT % CHUNK == 0

    inv_freq = ROPE_THETA ** (-jnp.arange(0, DIFF_DQK, 2, dtype=F32) / DIFF_DQK)
    invf = jnp.tile(inv_freq, LANES // (DIFF_DQK // 2))[None, :]
    sgn = jnp.tile(jnp.concatenate([-jnp.ones((DIFF_DQK // 2,), F32),
                                    jnp.ones((DIFF_DQK // 2,), F32)]), LANES // DIFF_DQK)[None, :]

    h = x.reshape(n, d)
    pos2 = positions.reshape(n, 1)
    for i in range(depth):
        wi = w_in[i]
        ga0 = 1536
        dv0 = ga0 + GLA_GATE_RANK + 1024
        w_in_r = jnp.concatenate(
            [wi[:, :ga0], wi[:, ga0 + GLA_GATE_RANK:dv0], wi[:, ga0:ga0 + GLA_GATE_RANK],
             jnp.zeros((d, LANES - GLA_GATE_RANK), F32)], axis=1).astype(BF16)
        w_dvt = wi[:, dv0:].T.astype(BF16)
        wa_pad = jnp.concatenate(
            [w_a_up[i], jnp.zeros((LANES - GLA_GATE_RANK, w_a_up.shape[2]), F32)], axis=0).astype(BF16)

        gq, gk, gv, gg, la, dq, dk, dvt = _inproj(
            h, pos2, norm_mix[i][None, :], w_in_r, w_dvt, wa_pad, b_a[i][None, :], invf, sgn)
        o_gla = _gla(gq, gk, gv, gg, la, gla_norm[i][None, :], batch, seq)

        lam_init = 0.8 - 0.6 * math.exp(-0.3 * i)
        lam_vecs = [v[i][None, :] for v in (lam_q1, lam_k1, lam_q2, lam_k2)]
        o_diff = _diff_attn(dq, dk, dvt, lam_vecs, diff_norm[i][None, :], lam_init, batch, seq)

        wu = w_up[i].astype(BF16)
        wg = wu[:, :d_ff].reshape(d, n_chunks, FF_CHUNK).transpose(1, 0, 2)
        wv = wu[:, d_ff:].reshape(d, n_chunks, FF_CHUNK).transpose(1, 0, 2)
        wd = w_down[i].astype(BF16)

        def chunked(a):
            r = a.shape[0]
            g = a[:, :d_ff].reshape(r, n_chunks, FF_CHUNK)
            v = a[:, d_ff:].reshape(r, n_chunks, FF_CHUNK)
            return jnp.concatenate([g, v], axis=2).transpose(1, 0, 2)

        cw = chunked(jnp.concatenate(
            [conv_w[i], jnp.zeros((SUBLANES - CONV_W, 2 * d_ff), F32)], axis=0))
        cb = chunked(conv_b[i][None, :])
        h = _tail(h, o_gla, o_diff, p[i].reshape(n, -1), w_out[i].astype(BF16),
                  norm_ffn[i][None, :], wg, wv, cw, cb, wd, norm_ple[i][None, :],
                  w_ple_gate[i].astype(BF16), w_ple_proj[i].astype(BF16),
                  norm_final[None, :], seq, i == depth - 1)
    return h.reshape(batch, seq, d)
```

```python
import functools
import math

import jax
import jax.numpy as jnp
from jax import lax
from jax.experimental import pallas as pl
from jax.experimental.pallas import tpu as pltpu

F32 = jnp.float32
BF16 = jnp.bfloat16

CHUNK = 64
ROPE_THETA = 10000.0
NORM_EPS = 1e-6
GLA_HEADS = 4
GLA_DK = 64
GLA_DV = 128
GLA_GATE_RANK = 16
GLA_GATE_NORM = 16.0
DIFF_HEADS = 4
DIFF_DQK = 64
DIFF_DV = 128
CONV_W = 3
NEG_INF = -1e30
MASK_VALUE = -(2.0 ** 100)
PERM_PITCH = 72
L_ROWS = 16
LOG2E = 1.4426950408889634
GELU_A = 2.0 * math.sqrt(2.0 / math.pi)
GELU_B = GELU_A * 0.044715

LANES = 128
SUBLANES = 8
VMEM_LIMIT_BYTES = 56 * 1024 * 1024

TM_IN = 512
TC_GLA = 512
TQ_ATT = 256
TK_ATT = 256
HB_ATT = 4
TM_OUT = 512
FF_CHUNK = 256


def _rms(x, g):
    return x * lax.rsqrt(jnp.mean(x * x, axis=-1, keepdims=True) + NORM_EPS) * g


def _dot(a, b):
    return jnp.dot(a, b, preferred_element_type=F32)


def _dot_nt(a, b):
    return lax.dot_general(a, b, (((1,), (1,)), ((), ())), preferred_element_type=F32)


def _dot_tn(a, b):
    return lax.dot_general(a, b, (((0,), (0,)), ((), ())), preferred_element_type=F32)


def _inproj_body(x_ref, pos_ref, g_ref, w_ref, wvt_ref, wa_ref, ba_ref, invf_ref, sgn_ref,
                 gq_ref, gk_ref, gv_ref, gg_ref, la_ref, dq_ref, dk_ref, dvt_ref):
    ang = pos_ref[...].astype(F32) * invf_ref[...]
    cos = jnp.cos(ang)
    sin = jnp.sin(ang) * sgn_ref[...]
    lane = lax.broadcasted_iota(jnp.int32, ang.shape, 1)
    first_half = (lane & (DIFF_DQK // 2)) == 0

    u = _rms(x_ref[...], g_ref[...]).astype(BF16)

    def proj(c0, c1):
        return _dot(u, w_ref[:, c0:c1])

    def rope(z, scale):
        outs = []
        for j in range(z.shape[1] // LANES):
            zj = z[:, j * LANES:(j + 1) * LANES]
            rot = jnp.where(first_half,
                            pltpu.roll(zj, LANES - DIFF_DQK // 2, 1),
                            pltpu.roll(zj, DIFF_DQK // 2, 1))
            outs.append(((zj * cos + rot * sin) * scale).astype(BF16))
        return jnp.concatenate(outs, axis=1)

    dq_ref[...] = rope(proj(1536, 2048), DIFF_DQK ** -0.5 * LOG2E)
    dk_ref[...] = rope(proj(2048, 2560), 1.0)

    a_low = proj(2560, 2688).astype(BF16)
    pre = _dot(a_low, wa_ref[...]) + ba_ref[...]
    log_sig = jnp.minimum(pre, 0.0) - jnp.log1p(jnp.exp(-jnp.abs(pre)))
    la_ref[...] = log_sig * (1.0 / GLA_GATE_NORM)

    gq_ref[...] = (proj(0, 256) * (GLA_DK ** -0.5)).astype(BF16)
    gk_ref[...] = proj(256, 512).astype(BF16)
    gv_ref[...] = proj(512, 1024).astype(BF16)
    gg_ref[...] = proj(1024, 1536).astype(BF16)
    dvt = _dot_nt(wvt_ref[...], u)
    tk = dvt_ref.shape[2]
    for c in range(dvt_ref.shape[0]):
        dvt_ref[c] = dvt[:, c * tk:(c + 1) * tk].astype(BF16)


def _inproj(x2, pos2, norm_mix, w_in_r, w_dvt, wa_pad, b_a, invf, sgn):
    n, d = x2.shape
    tm = TM_IN
    row = lambda i: (i, 0)
    const = lambda i: (0, 0)
    widths = (256, 256, 512, 512, 256, 512, 512)
    dtypes = (BF16, BF16, BF16, BF16, F32, BF16, BF16)
    tk = TK_ATT
    hv = w_dvt.shape[0]
    return pl.pallas_call(
        _inproj_body,
        out_shape=[jax.ShapeDtypeStruct((n, w), dt) for w, dt in zip(widths, dtypes)]
        + [jax.ShapeDtypeStruct((n // tk, hv, tk), BF16)],
        grid=(n // tm,),
        in_specs=[
            pl.BlockSpec((tm, d), row),
            pl.BlockSpec((tm, 1), row),
            pl.BlockSpec((1, d), const),
            pl.BlockSpec(w_in_r.shape, const),
            pl.BlockSpec(w_dvt.shape, const),
            pl.BlockSpec(wa_pad.shape, const),
            pl.BlockSpec((1, 256), const),
            pl.BlockSpec((1, LANES), const),
            pl.BlockSpec((1, LANES), const),
        ],
        out_specs=[pl.BlockSpec((tm, w), row) for w in widths]
        + [pl.BlockSpec((tm // tk, hv, tk), lambda i: (i, 0, 0))],
        compiler_params=pltpu.CompilerParams(
            dimension_semantics=("arbitrary",), vmem_limit_bytes=VMEM_LIMIT_BYTES),
        name="inproj",
    )(x2, pos2, norm_mix, w_in_r, w_dvt, wa_pad, b_a, invf, sgn)


def _gla_body(q_ref, k_ref, v_ref, g_ref, la_ref, gn_ref, o_ref, st_ref):
    @pl.when(pl.program_id(1) == 0)
    def _():
        st_ref[...] = jnp.zeros_like(st_ref)

    hk = GLA_HEADS * GLA_DK
    r_i = lax.broadcasted_iota(jnp.int32, (CHUNK, CHUNK), 0)
    c_i = lax.broadcasted_iota(jnp.int32, (CHUNK, CHUNK), 1)
    tril = (r_i >= c_i).astype(BF16)
    lane_head = lax.broadcasted_iota(jnp.int32, (1, hk), 1) // GLA_DK
    head_mask = [lane_head == h for h in range(GLA_HEADS)]
    rr = lax.broadcasted_iota(jnp.int32, (GLA_HEADS * CHUNK, CHUNK), 0) % CHUNK
    cc = lax.broadcasted_iota(jnp.int32, (GLA_HEADS * CHUNK, CHUNK), 1)
    tri_stack = rr >= cc
    gn = gn_ref[...]

    n_chunks = q_ref.shape[0] // CHUNK
    chunks = range(n_chunks)
    rows_of = [slice(c * CHUNK, (c + 1) * CHUNK) for c in chunks]
    head_rows = [slice(h * CHUNK, (h + 1) * CHUNK) for h in range(GLA_HEADS)]
    head_keys = [slice(h * GLA_DK, (h + 1) * GLA_DK) for h in range(GLA_HEADS)]
    head_vals = [slice(h * GLA_DV, (h + 1) * GLA_DV) for h in range(GLA_HEADS)]

    def cumsum(c):
        la = la_ref[rows_of[c], :]
        la_hi = la.astype(BF16)
        la_lo = (la - la_hi.astype(F32)).astype(BF16)
        return _dot(tril, la_hi) + _dot(tril, la_lo)

    bs = [cumsum(c) for c in chunks]

    def gated(c):
        b = bs[c]
        eb = jnp.exp(b)
        enb = jnp.exp(-b)
        q = q_ref[rows_of[c], :].astype(F32)
        k = k_ref[rows_of[c], :].astype(F32)
        qf = q * eb
        qb = q * enb
        qf_st = jnp.concatenate([jnp.where(m, qf, 0.0) for m in head_mask], axis=0).astype(BF16)
        qb_st = jnp.concatenate([jnp.where(m, qb, 0.0) for m in head_mask], axis=0).astype(BF16)
        k_dec = (k * jnp.exp(b[CHUNK - 1:CHUNK, :] - b)).astype(BF16)
        decay = jnp.exp(b[CHUNK - SUBLANES:, :].T[:, SUBLANES - 1:SUBLANES])
        return qf_st, qb_st, (k * enb).astype(BF16), (k * eb).astype(BF16), k_dec.T, decay

    gs = [gated(c) for c in chunks]
    a_fwd = [_dot_nt(gs[c][0], gs[c][2]) for c in chunks]
    a_bwd = [_dot_nt(gs[c][1], gs[c][3]) for c in chunks]
    a = [jnp.where(tri_stack, a_fwd[c], a_bwd[c]).astype(BF16) for c in chunks]
    o_intra = [[_dot(a[c][head_rows[h], :], v_ref[rows_of[c], head_vals[h]])
                for h in range(GLA_HEADS)] for c in chunks]
    d_st = [jnp.concatenate([_dot(gs[c][4][head_keys[h], :], v_ref[rows_of[c], head_vals[h]])
                             for h in range(GLA_HEADS)], axis=0) for c in chunks]

    st = st_ref[...]
    states = []
    for c in chunks:
        states.append(st.astype(BF16))
        st = st * gs[c][5] + d_st[c]
    st_ref[...] = st

    o_inter = [_dot(gs[c][0], states[c]) for c in chunks]
    for c in chunks:
        rows = rows_of[c]
        for h in range(GLA_HEADS):
            vs = head_vals[h]
            o = o_intra[c][h] + o_inter[c][head_rows[h], :]
            o = _rms(o, gn[:, vs])
            g = g_ref[rows, vs].astype(F32)
            o_ref[rows, vs] = (o * (g * jax.nn.sigmoid(g))).astype(BF16)


def _gla(gq, gk, gv, gg, la, gla_norm, batch, seq):
    tc = TC_GLA
    nt = seq // tc
    row = lambda b, t: (b * nt + t, 0)
    const = lambda b, t: (0, 0)
    hk = GLA_HEADS * GLA_DK
    hv = GLA_HEADS * GLA_DV
    return pl.pallas_call(
        _gla_body,
        out_shape=jax.ShapeDtypeStruct((batch * seq, hv), BF16),
        grid=(batch, nt),
        in_specs=[
            pl.BlockSpec((tc, hk), row),
            pl.BlockSpec((tc, hk), row),
            pl.BlockSpec((tc, hv), row),
            pl.BlockSpec((tc, hv), row),
            pl.BlockSpec((tc, hk), row),
            pl.BlockSpec((1, hv), const),
        ],
        out_specs=pl.BlockSpec((tc, hv), row),
        scratch_shapes=[pltpu.VMEM((hk, GLA_DV), F32)],
        compiler_params=pltpu.CompilerParams(
            dimension_semantics=("arbitrary", "arbitrary"), vmem_limit_bytes=VMEM_LIMIT_BYTES),
        name="gla",
    )(gq, gk, gv, gg, la, gla_norm)


def _diff_body(lam_init, lq1_ref, lk1_ref, lq2_ref, lk2_ref, q_ref, k_ref, vt_ref, gn_ref,
               o_ref, m_ref, acc_ref, s_ref, mb_ref):
    tq = q_ref.shape[0]
    tk = vt_ref.shape[2]
    hb = q_ref.shape[1] // LANES
    qi = pl.program_id(2)

    lane = lax.broadcasted_iota(jnp.int32, (tq, LANES), 1)
    qs = []
    for a in range(hb):
        q = q_ref[:, a * LANES:(a + 1) * LANES]
        zero = jnp.zeros_like(q)
        qs.append(jnp.where(lane < DIFF_DQK, q, zero))
        qs.append(jnp.where(lane >= DIFF_DQK, q, zero))
    nc = 2 * hb

    heads = [slice((c // 2) * LANES, (c // 2 + 1) * LANES) for c in range(nc)]

    m_ref[...] = jnp.full_like(m_ref, MASK_VALUE)
    acc_ref[...] = jnp.zeros_like(acc_ref)
    ones_rows = jnp.ones((L_ROWS, tk), BF16)

    def scores(t, c, masked):
        k0 = pl.multiple_of(t * tk, tk)
        s_new = _dot_nt(k_ref[pl.ds(k0, tk), heads[c]], qs[c])
        if masked:
            k_chunk = t * (tk // CHUNK) + lax.broadcasted_iota(jnp.int32, (tk, tq), 0) // CHUNK
            q_chunk = qi * (tq // CHUNK) + lax.broadcasted_iota(jnp.int32, (tk, tq), 1) // CHUNK
            s_new = jnp.where(k_chunk <= q_chunk, s_new, MASK_VALUE)
        return s_new

    def keep_scores(c, s_new):
        sb = s_new.astype(BF16)
        s_ref[c] = sb
        mb_ref[c] = jnp.max(sb, axis=0, keepdims=True).astype(F32)

    def softmax_pv(t, c):
        m_prev = m_ref[c]
        m_next = jnp.maximum(m_prev, mb_ref[c])
        alpha = jnp.exp2(m_prev - m_next)
        p = jnp.exp2(s_ref[c] - m_next.astype(BF16))
        m_ref[c] = m_next
        v1 = jnp.concatenate([vt_ref[t, heads[c], :], ones_rows], axis=0)
        return alpha, _dot(v1, p)

    def both(t, masked):
        for c in range(nc):
            alpha, pv = softmax_pv(t - 1, c)
            s_new = scores(t, c, masked)
            acc_ref[c] = alpha * acc_ref[c] + pv
            keep_scores(c, s_new)

    for c in range(nc):
        keep_scores(c, scores(0, c, True))

    @pl.when(qi > 0)
    def _():
        def pair(i, carry):
            both(2 * i + 1, False)
            both(2 * i + 2, False)
            return carry
        n_pairs = (qi - 1) // 2
        lax.fori_loop(0, n_pairs, pair, 0)

        @pl.when((qi - 1) % 2 == 1)
        def _():
            both(qi - 1, False)

        both(qi, True)

    for c in range(nc):
        alpha, pv = softmax_pv(qi, c)
        acc_ref[c] = alpha * acc_ref[c] + pv

    lam = (jnp.exp(jnp.sum(lq1_ref[...] * lk1_ref[...], axis=1, keepdims=True))
           - jnp.exp(jnp.sum(lq2_ref[...] * lk2_ref[...], axis=1, keepdims=True))
           + lam_init)
    for a in range(hb):
        hs = slice(a * LANES, (a + 1) * LANES)
        acc0 = acc_ref[2 * a]
        acc1 = acc_ref[2 * a + 1]
        o0 = acc0[:DIFF_DV, :] * (1.0 / acc0[DIFF_DV:DIFF_DV + 1, :])
        o1 = acc1[:DIFF_DV, :] * (1.0 / acc1[DIFF_DV:DIFF_DV + 1, :])
        o = (o0 - lam * o1).T
        o = _rms(o, gn_ref[:, hs]) * (1.0 - lam_init)
        o_ref[:, hs] = o.astype(BF16)


def _diff_attn(dq, dk, dvt, lam_vecs, diff_norm, lam_init, batch, seq):
    tq = TQ_ATT
    tk = dvt.shape[2]
    assert tq == tk
    nq = seq // tq
    hb = HB_ATT
    hv = DIFF_HEADS * DIFF_DV
    vec = pl.BlockSpec((1, DIFF_DQK), lambda b, h, i: (0, 0))
    return pl.pallas_call(
        functools.partial(_diff_body, lam_init),
        out_shape=jax.ShapeDtypeStruct((batch * seq, hv), BF16),
        grid=(batch, DIFF_HEADS // hb, nq),
        in_specs=[
            vec, vec, vec, vec,
            pl.BlockSpec((tq, hb * LANES), lambda b, h, i: (b * nq + i, h)),
            pl.BlockSpec((seq, hb * LANES), lambda b, h, i: (b, h)),
            pl.BlockSpec((seq // tk, hb * DIFF_DV, tk), lambda b, h, i: (b, h, 0)),
            pl.BlockSpec((1, hb * DIFF_DV), lambda b, h, i: (0, h)),
        ],
        out_specs=pl.BlockSpec((tq, hb * DIFF_DV), lambda b, h, i: (b * nq + i, h)),
        scratch_shapes=[
            pltpu.VMEM((2 * hb, 1, tq), F32),
            pltpu.VMEM((2 * hb, DIFF_DV + L_ROWS, tq), F32),
            pltpu.VMEM((2 * hb, tk, tq), BF16),
            pltpu.VMEM((2 * hb, 1, tq), F32),
        ],
        compiler_params=pltpu.CompilerParams(
            dimension_semantics=("arbitrary", "arbitrary", "arbitrary"),
            vmem_limit_bytes=VMEM_LIMIT_BYTES),
        name="diff_attn",
    )(*lam_vecs, dq, dk, dvt, diff_norm)


def _tail_body(tiles_per_seq, final, x_ref, og_ref, od_ref, p_ref, wo_ref, nf_ref, wg_ref, wv_ref,
               cw_ref, cb_ref, wd_ref, npl_ref, wpg_ref, wpp_ref, nfin_ref,
               y_ref, carry_ref, perm_ref, act_ref):
    tm, d = x_ref.shape
    n_chunks = wg_ref.shape[0]
    c = FF_CHUNK
    nseg = SUBLANES
    seg = tm // nseg
    nl = d // LANES

    @pl.when(pl.program_id(0) % tiles_per_seq == 0)
    def _():
        carry_ref[...] = jnp.zeros_like(carry_ref)

    half = og_ref.shape[1]
    h1 = (x_ref[...] + _dot(og_ref[...], wo_ref[0:half, :])
          + _dot(od_ref[...], wo_ref[half:2 * half, :]))

    def permute(a):
        for l in range(nl):
            for s in range(nseg):
                perm_ref[l, s * PERM_PITCH:s * PERM_PITCH + seg, :] = (
                    a[s * seg:(s + 1) * seg, l * LANES:(l + 1) * LANES])
        return jnp.concatenate(
            [jnp.concatenate([perm_ref[l, pl.ds(v, nseg, stride=PERM_PITCH), :] for v in range(seg)],
                             axis=0) for l in range(nl)], axis=1)

    def unpermute(a):
        for l in range(nl):
            for v in range(seg):
                perm_ref[l, pl.ds(v, nseg, stride=PERM_PITCH), :] = (
                    a[v * nseg:(v + 1) * nseg, l * LANES:(l + 1) * LANES])
        return jnp.concatenate(
            [jnp.concatenate([perm_ref[l, s * PERM_PITCH:s * PERM_PITCH + seg, :] for s in range(nseg)],
                             axis=0) for l in range(nl)], axis=1)

    u = permute(_rms(h1, nf_ref[...])).astype(BF16)

    def up(j):
        return jnp.concatenate([_dot(u, wg_ref[j]), _dot(u, wv_ref[j])], axis=1)

    row = lax.broadcasted_iota(jnp.int32, (SUBLANES, 2 * c), 0)

    def wrapped(last_block, prev_last_block):
        return pltpu.roll(jnp.where(row == SUBLANES - 1, prev_last_block, last_block), 1, 0)

    z_next = up(0)
    for j in range(n_chunks):
        z = z_next
        if j + 1 < n_chunks:
            z_next = up(j + 1)
        prev = carry_ref[j]
        carry_ref[j] = z[tm - 2 * SUBLANES:, :]
        b1 = wrapped(z[tm - SUBLANES:, :], prev[SUBLANES:, :])
        b2 = wrapped(z[tm - 2 * SUBLANES:tm - SUBLANES, :], prev[:SUBLANES, :])
        s1 = jnp.concatenate([b1, z[:tm - SUBLANES, :]], axis=0)
        s2 = jnp.concatenate([b2, b1, z[:tm - 2 * SUBLANES, :]], axis=0)
        cw = cw_ref[j]
        cv = cb_ref[j] + s2 * cw[0:1, :] + s1 * cw[1:2, :] + z * cw[2:3, :]
        gate = cv[:, :c]
        y2 = gate * (GELU_A + GELU_B * (gate * gate))
        act_ref[:, j * c:(j + 1) * c] = (gate / (1.0 + jnp.exp(-y2)) * cv[:, c:]).astype(BF16)

    h2 = h1 + unpermute(_dot(act_ref[...], wd_ref[...]))
    gate = jax.nn.sigmoid(_dot(_rms(h2, npl_ref[...]).astype(BF16), wpg_ref[...]))
    h3 = h2 + gate * _dot(p_ref[...].astype(BF16), wpp_ref[...])
    y_ref[...] = _rms(h3, nfin_ref[...]) if final else h3


def _tail(x2, o_gla, o_diff, p2, w_out, norm_ffn, wg, wv, cw, cb, wd, norm_ple, w_pg, w_pp,
          norm_final, seq, final):
    n, d = x2.shape
    tm = TM_OUT
    row = lambda i: (i, 0)
    c2 = lambda i: (0, 0)
    c3 = lambda i: (0, 0, 0)
    one = pl.Buffered(1)

    def resident(a):
        return pl.BlockSpec(a.shape, c3 if a.ndim == 3 else c2, pipeline_mode=one)

    n_chunks = wg.shape[0]
    return pl.pallas_call(
        functools.partial(_tail_body, seq // tm, final),
        out_shape=jax.ShapeDtypeStruct((n, d), F32),
        grid=(n // tm,),
        in_specs=[
            pl.BlockSpec((tm, d), row),
            pl.BlockSpec((tm, o_gla.shape[1]), row),
            pl.BlockSpec((tm, o_diff.shape[1]), row),
            pl.BlockSpec((tm, p2.shape[1]), row),
            resident(w_out), resident(norm_ffn), resident(wg), resident(wv), resident(cw),
            resident(cb), resident(wd), resident(norm_ple), resident(w_pg), resident(w_pp),
            resident(norm_final),
        ],
        out_specs=pl.BlockSpec((tm, d), row),
        scratch_shapes=[
            pltpu.VMEM((n_chunks, 2 * SUBLANES, 2 * FF_CHUNK), F32),
            pltpu.VMEM((d // LANES, SUBLANES * PERM_PITCH, LANES), F32),
            pltpu.VMEM((tm, n_chunks * FF_CHUNK), BF16),
        ],
        compiler_params=pltpu.CompilerParams(
            dimension_semantics=("arbitrary",), vmem_limit_bytes=VMEM_LIMIT_BYTES),
        name="tail",
    )(x2, o_gla, o_diff, p2, w_out, norm_ffn, wg, wv, cw, cb, wd, norm_ple, w_pg, w_pp,
      norm_final)


def kernel(x, p, positions, norm_mix, w_in, w_a_up, b_a, gla_norm, lam_q1, lam_k1, lam_q2, lam_k2,
           diff_norm, w_out, norm_ffn, w_up, conv_w, conv_b, w_down, norm_ple, w_ple_gate,
           w_ple_proj, norm_final):
    batch, seq, d = x.shape
    depth = w_in.shape[0]
    n = batch * seq
    d_ff = w_down.shape[1]
    n_chunks = d_ff // FF_CHUNK
    assert seq % max(TM_IN, TC_GLA, TQ_ATT, TM_OUT) == 0 and TM_IN % TK_ATT == 0
    assert d_ff % FF_CHUNK == 0 and TK_ATT % CHUNK == 0

    inv_freq = ROPE_THETA ** (-jnp.arange(0, DIFF_DQK, 2, dtype=F32) / DIFF_DQK)
    invf = jnp.tile(inv_freq, LANES // (DIFF_DQK // 2))[None, :]
    sgn = jnp.tile(jnp.concatenate([-jnp.ones((DIFF_DQK // 2,), F32),
                                    jnp.ones((DIFF_DQK // 2,), F32)]), LANES // DIFF_DQK)[None, :]

    h = x.reshape(n, d)
    pos2 = positions.reshape(n, 1)
    for i in range(depth):
        wi = w_in[i]
        ga0 = 1536
        dv0 = ga0 + GLA_GATE_RANK + 1024
        w_in_r = jnp.concatenate(
            [wi[:, :ga0], wi[:, ga0 + GLA_GATE_RANK:dv0], wi[:, ga0:ga0 + GLA_GATE_RANK],
             jnp.zeros((d, LANES - GLA_GATE_RANK), F32)], axis=1).astype(BF16)
        w_dvt = wi[:, dv0:].T.astype(BF16)
        wa_pad = jnp.concatenate(
            [w_a_up[i], jnp.zeros((LANES - GLA_GATE_RANK, w_a_up.shape[2]), F32)], axis=0).astype(BF16)

        gq, gk, gv, gg, la, dq, dk, dvt = _inproj(
            h, pos2, norm_mix[i][None, :], w_in_r, w_dvt, wa_pad, b_a[i][None, :], invf, sgn)
        o_gla = _gla(gq, gk, gv, gg, la, gla_norm[i][None, :], batch, seq)

        lam_init = 0.8 - 0.6 * math.exp(-0.3 * i)
        lam_vecs = [v[i][None, :] for v in (lam_q1, lam_k1, lam_q2, lam_k2)]
        o_diff = _diff_attn(dq, dk, dvt, lam_vecs, diff_norm[i][None, :], lam_init, batch, seq)

        wu = w_up[i].astype(BF16)
        wg = wu[:, :d_ff].reshape(d, n_chunks, FF_CHUNK).transpose(1, 0, 2)
        wv = wu[:, d_ff:].reshape(d, n_chunks, FF_CHUNK).transpose(1, 0, 2)
        wd = w_down[i].astype(BF16)

        def chunked(a):
            r = a.shape[0]
            g = a[:, :d_ff].reshape(r, n_chunks, FF_CHUNK)
            v = a[:, d_ff:].reshape(r, n_chunks, FF_CHUNK)
            return jnp.concatenate([g, v], axis=2).transpose(1, 0, 2)

        cw = chunked(jnp.concatenate(
            [conv_w[i], jnp.zeros((SUBLANES - CONV_W, 2 * d_ff), F32)], axis=0))
        cb = chunked(conv_b[i][None, :])
        h = _tail(h, o_gla, o_diff, p[i].reshape(n, -1), w_out[i].astype(BF16),
                  norm_ffn[i][None, :], wg, wv, cw, cb, wd, norm_ple[i][None, :],
                  w_ple_gate[i].astype(BF16), w_ple_proj[i].astype(BF16),
                  norm_final[None, :], seq, i == depth - 1)
    return h.reshape(batch, seq, d)
```

```python
import functools
import math

import jax
import jax.numpy as jnp
from jax import lax
from jax.experimental import pallas as pl
from jax.experimental.pallas import tpu as pltpu

F32 = jnp.float32
BF16 = jnp.bfloat16

CHUNK = 64
ROPE_THETA = 10000.0
NORM_EPS = 1e-6
GLA_HEADS = 4
GLA_DK = 64
GLA_DV = 128
GLA_GATE_RANK = 16
GLA_GATE_NORM = 16.0
DIFF_HEADS = 4
DIFF_DQK = 64
DIFF_DV = 128
CONV_W = 3
NEG_INF = -1e30
MASK_VALUE = -(2.0 ** 100)
PERM_PITCH = 72
L_ROWS = 16
LOG2E = 1.4426950408889634
GELU_A = 2.0 * math.sqrt(2.0 / math.pi)
GELU_B = GELU_A * 0.044715

LANES = 128
SUBLANES = 8
VMEM_LIMIT_BYTES = 56 * 1024 * 1024

TM_IN = 512
TC_GLA = 512
TQ_ATT = 256
TK_ATT = 256
HB_ATT = 4
TM_OUT = 512
FF_CHUNK = 256


def _rms(x, g):
    return x * lax.rsqrt(jnp.mean(x * x, axis=-1, keepdims=True) + NORM_EPS) * g


def _dot(a, b):
    return jnp.dot(a, b, preferred_element_type=F32)


def _dot_nt(a, b):
    return lax.dot_general(a, b, (((1,), (1,)), ((), ())), preferred_element_type=F32)


def _dot_tn(a, b):
    return lax.dot_general(a, b, (((0,), (0,)), ((), ())), preferred_element_type=F32)


def _inproj_body(x_ref, pos_ref, g_ref, w_ref, wvt_ref, wa_ref, ba_ref, invf_ref, sgn_ref,
                 gq_ref, gk_ref, gv_ref, gg_ref, la_ref, dq_ref, dk_ref, dvt_ref):
    tm = pos_ref.shape[0]
    nfreq = DIFF_DQK // 2
    assert tm == LANES * (LANES // nfreq)
    pos = pos_ref[...].astype(F32)
    lane_group = lax.broadcasted_iota(jnp.int32, (LANES, LANES), 1) // nfreq
    pos_packed = pos[0:LANES, :]
    for jg in range(1, LANES // nfreq):
        pos_packed = jnp.where(lane_group == jg, pos[jg * LANES:(jg + 1) * LANES, :], pos_packed)
    ang = pos_packed * invf_ref[...]

    def spread(packed):
        blocks = []
        for jg in range(LANES // nfreq):
            x = jnp.where(lane_group == jg, packed, 0.0)
            x = x + pltpu.roll(x, 2 * nfreq, 1)
            blocks.append(x + pltpu.roll(x, nfreq, 1))
        return jnp.concatenate(blocks, axis=0)

    cos = spread(jnp.cos(ang))
    sin = spread(jnp.sin(ang)) * sgn_ref[...]
    lane = lax.broadcasted_iota(jnp.int32, (tm, LANES), 1)
    first_half = (lane & nfreq) == 0

    u = _rms(x_ref[...], g_ref[...]).astype(BF16)

    def proj(c0, c1):
        return _dot(u, w_ref[:, c0:c1])

    def rope(z, scale):
        outs = []
        for j in range(z.shape[1] // LANES):
            zj = z[:, j * LANES:(j + 1) * LANES]
            rot = jnp.where(first_half,
                            pltpu.roll(zj, LANES - DIFF_DQK // 2, 1),
                            pltpu.roll(zj, DIFF_DQK // 2, 1))
            outs.append(((zj * cos + rot * sin) * scale).astype(BF16))
        return jnp.concatenate(outs, axis=1)

    dq_ref[...] = rope(proj(1536, 2048), DIFF_DQK ** -0.5 * LOG2E)
    dk_ref[...] = rope(proj(2048, 2560), 1.0)

    a_low = proj(2560, 2688).astype(BF16)
    pre = _dot(a_low, wa_ref[...]) + ba_ref[...]
    log_sig = jnp.minimum(pre, 0.0) - jnp.log1p(jnp.exp(-jnp.abs(pre)))
    la_ref[...] = log_sig * (1.0 / GLA_GATE_NORM)

    gq_ref[...] = (proj(0, 256) * (GLA_DK ** -0.5)).astype(BF16)
    gk_ref[...] = proj(256, 512).astype(BF16)
    gv_ref[...] = proj(512, 1024).astype(BF16)
    gg_ref[...] = proj(1024, 1536).astype(BF16)
    dvt = _dot_nt(wvt_ref[...], u)
    tk = dvt_ref.shape[2]
    for c in range(dvt_ref.shape[0]):
        dvt_ref[c] = dvt[:, c * tk:(c + 1) * tk].astype(BF16)


def _inproj(x2, pos2, norm_mix, w_in_r, w_dvt, wa_pad, b_a, invf, sgn):
    n, d = x2.shape
    tm = TM_IN
    row = lambda i: (i, 0)
    const = lambda i: (0, 0)
    widths = (256, 256, 512, 512, 256, 512, 512)
    dtypes = (BF16, BF16, BF16, BF16, F32, BF16, BF16)
    tk = TK_ATT
    hv = w_dvt.shape[0]
    return pl.pallas_call(
        _inproj_body,
        out_shape=[jax.ShapeDtypeStruct((n, w), dt) for w, dt in zip(widths, dtypes)]
        + [jax.ShapeDtypeStruct((n // tk, hv, tk), BF16)],
        grid=(n // tm,),
        in_specs=[
            pl.BlockSpec((tm, d), row),
            pl.BlockSpec((tm, 1), row),
            pl.BlockSpec((1, d), const),
            pl.BlockSpec(w_in_r.shape, const),
            pl.BlockSpec(w_dvt.shape, const),
            pl.BlockSpec(wa_pad.shape, const),
            pl.BlockSpec((1, 256), const),
            pl.BlockSpec((1, LANES), const),
            pl.BlockSpec((1, LANES), const),
        ],
        out_specs=[pl.BlockSpec((tm, w), row) for w in widths]
        + [pl.BlockSpec((tm // tk, hv, tk), lambda i: (i, 0, 0))],
        compiler_params=pltpu.CompilerParams(
            dimension_semantics=("arbitrary",), vmem_limit_bytes=VMEM_LIMIT_BYTES),
        name="inproj",
    )(x2, pos2, norm_mix, w_in_r, w_dvt, wa_pad, b_a, invf, sgn)


def _gla_body(q_ref, k_ref, v_ref, g_ref, la_ref, gn_ref, o_ref, st_ref):
    @pl.when(pl.program_id(1) == 0)
    def _():
        st_ref[...] = jnp.zeros_like(st_ref)

    hk = GLA_HEADS * GLA_DK
    r_i = lax.broadcasted_iota(jnp.int32, (CHUNK, CHUNK), 0)
    c_i = lax.broadcasted_iota(jnp.int32, (CHUNK, CHUNK), 1)
    tril = (r_i >= c_i).astype(BF16)
    lane_head = lax.broadcasted_iota(jnp.int32, (1, hk), 1) // GLA_DK
    head_mask = [lane_head == h for h in range(GLA_HEADS)]
    rr = lax.broadcasted_iota(jnp.int32, (GLA_HEADS * CHUNK, CHUNK), 0) % CHUNK
    cc = lax.broadcasted_iota(jnp.int32, (GLA_HEADS * CHUNK, CHUNK), 1)
    tri_stack = rr >= cc
    gn = gn_ref[...]

    n_chunks = q_ref.shape[0] // CHUNK
    chunks = range(n_chunks)
    rows_of = [slice(c * CHUNK, (c + 1) * CHUNK) for c in chunks]
    head_rows = [slice(h * CHUNK, (h + 1) * CHUNK) for h in range(GLA_HEADS)]
    head_keys = [slice(h * GLA_DK, (h + 1) * GLA_DK) for h in range(GLA_HEADS)]
    head_vals = [slice(h * GLA_DV, (h + 1) * GLA_DV) for h in range(GLA_HEADS)]

    def cumsum(c):
        la = la_ref[rows_of[c], :]
        la_hi = la.astype(BF16)
        la_lo = (la - la_hi.astype(F32)).astype(BF16)
        return _dot(tril, la_hi) + _dot(tril, la_lo)

    bs = [cumsum(c) for c in chunks]

    def gated(c):
        b = bs[c]
        eb = jnp.exp(b)
        enb = jnp.exp(-b)
        q = q_ref[rows_of[c], :].astype(F32)
        k = k_ref[rows_of[c], :].astype(F32)
        qf = q * eb
        qb = q * enb
        qf_st = jnp.concatenate([jnp.where(m, qf, 0.0) for m in head_mask], axis=0).astype(BF16)
        qb_st = jnp.concatenate([jnp.where(m, qb, 0.0) for m in head_mask], axis=0).astype(BF16)
        k_dec = (k * jnp.exp(b[CHUNK - 1:CHUNK, :] - b)).astype(BF16)
        decay = jnp.exp(b[CHUNK - SUBLANES:, :].T[:, SUBLANES - 1:SUBLANES])
        return qf_st, qb_st, (k * enb).astype(BF16), (k * eb).astype(BF16), k_dec.T, decay

    gs = [gated(c) for c in chunks]
    a_fwd = [_dot_nt(gs[c][0], gs[c][2]) for c in chunks]
    a_bwd = [_dot_nt(gs[c][1], gs[c][3]) for c in chunks]
    a = [jnp.where(tri_stack, a_fwd[c], a_bwd[c]).astype(BF16) for c in chunks]
    o_intra = [[_dot(a[c][head_rows[h], :], v_ref[rows_of[c], head_vals[h]])
                for h in range(GLA_HEADS)] for c in chunks]
    d_st = [jnp.concatenate([_dot(gs[c][4][head_keys[h], :], v_ref[rows_of[c], head_vals[h]])
                             for h in range(GLA_HEADS)], axis=0) for c in chunks]

    st = st_ref[...]
    states = []
    for c in chunks:
        states.append(st.astype(BF16))
        st = st * gs[c][5] + d_st[c]
    st_ref[...] = st

    o_inter = [_dot(gs[c][0], states[c]) for c in chunks]
    for c in chunks:
        rows = rows_of[c]
        for h in range(GLA_HEADS):
            vs = head_vals[h]
            o = o_intra[c][h] + o_inter[c][head_rows[h], :]
            o = _rms(o, gn[:, vs])
            g = g_ref[rows, vs].astype(F32)
            o_ref[rows, vs] = (o * (g * jax.nn.sigmoid(g))).astype(BF16)


def _gla(gq, gk, gv, gg, la, gla_norm, batch, seq):
    tc = TC_GLA
    nt = seq // tc
    row = lambda b, t: (b * nt + t, 0)
    const = lambda b, t: (0, 0)
    hk = GLA_HEADS * GLA_DK
    hv = GLA_HEADS * GLA_DV
    return pl.pallas_call(
        _gla_body,
        out_shape=jax.ShapeDtypeStruct((batch * seq, hv), BF16),
        grid=(batch, nt),
        in_specs=[
            pl.BlockSpec((tc, hk), row),
            pl.BlockSpec((tc, hk), row),
            pl.BlockSpec((tc, hv), row),
            pl.BlockSpec((tc, hv), row),
            pl.BlockSpec((tc, hk), row),
            pl.BlockSpec((1, hv), const),
        ],
        out_specs=pl.BlockSpec((tc, hv), row),
        scratch_shapes=[pltpu.VMEM((hk, GLA_DV), F32)],
        compiler_params=pltpu.CompilerParams(
            dimension_semantics=("arbitrary", "arbitrary"), vmem_limit_bytes=VMEM_LIMIT_BYTES),
        name="gla",
    )(gq, gk, gv, gg, la, gla_norm)


def _diff_body(lam_init, lq1_ref, lk1_ref, lq2_ref, lk2_ref, q_ref, k_ref, vt_ref, gn_ref,
               o_ref, m_ref, acc_ref, s_ref, mb_ref):
    tq = q_ref.shape[0]
    tk = vt_ref.shape[2]
    hb = q_ref.shape[1] // LANES
    qi = pl.program_id(2)

    lane = lax.broadcasted_iota(jnp.int32, (tq, LANES), 1)
    qs = []
    for a in range(hb):
        q = q_ref[:, a * LANES:(a + 1) * LANES]
        zero = jnp.zeros_like(q)
        qs.append(jnp.where(lane < DIFF_DQK, q, zero))
        qs.append(jnp.where(lane >= DIFF_DQK, q, zero))
    nc = 2 * hb

    heads = [slice((c // 2) * LANES, (c // 2 + 1) * LANES) for c in range(nc)]

    m_ref[...] = jnp.full_like(m_ref, MASK_VALUE)
    acc_ref[...] = jnp.zeros_like(acc_ref)
    ones_rows = jnp.ones((L_ROWS, tk), BF16)

    def scores(t, c, masked):
        k0 = pl.multiple_of(t * tk, tk)
        s_new = _dot_nt(k_ref[pl.ds(k0, tk), heads[c]], qs[c])
        if masked:
            k_chunk = t * (tk // CHUNK) + lax.broadcasted_iota(jnp.int32, (tk, tq), 0) // CHUNK
            q_chunk = qi * (tq // CHUNK) + lax.broadcasted_iota(jnp.int32, (tk, tq), 1) // CHUNK
            s_new = jnp.where(k_chunk <= q_chunk, s_new, MASK_VALUE)
        return s_new

    def keep_scores(c, s_new):
        sb = s_new.astype(BF16)
        s_ref[c] = sb
        mb_ref[c] = jnp.max(sb, axis=0, keepdims=True).astype(F32)

    def softmax_pv(t, c):
        m_prev = m_ref[c]
        m_next = jnp.maximum(m_prev, mb_ref[c])
        alpha = jnp.exp2(m_prev - m_next)
        p = jnp.exp2(s_ref[c] - m_next.astype(BF16))
        m_ref[c] = m_next
        v1 = jnp.concatenate([vt_ref[t, heads[c], :], ones_rows], axis=0)
        return alpha, _dot(v1, p)

    def both(t, masked):
        for c in range(nc):
            alpha, pv = softmax_pv(t - 1, c)
            s_new = scores(t, c, masked)
            acc_ref[c] = alpha * acc_ref[c] + pv
            keep_scores(c, s_new)

    for c in range(nc):
        keep_scores(c, scores(0, c, True))

    @pl.when(qi > 0)
    def _():
        def pair(i, carry):
            both(2 * i + 1, False)
            both(2 * i + 2, False)
            return carry
        n_pairs = (qi - 1) // 2
        lax.fori_loop(0, n_pairs, pair, 0)

        @pl.when((qi - 1) % 2 == 1)
        def _():
            both(qi - 1, False)

        both(qi, True)

    for c in range(nc):
        alpha, pv = softmax_pv(qi, c)
        acc_ref[c] = alpha * acc_ref[c] + pv

    lam = (jnp.exp(jnp.sum(lq1_ref[...] * lk1_ref[...], axis=1, keepdims=True))
           - jnp.exp(jnp.sum(lq2_ref[...] * lk2_ref[...], axis=1, keepdims=True))
           + lam_init)
    for a in range(hb):
        hs = slice(a * LANES, (a + 1) * LANES)
        acc0 = acc_ref[2 * a]
        acc1 = acc_ref[2 * a + 1]
        o0 = acc0[:DIFF_DV, :] * (1.0 / acc0[DIFF_DV:DIFF_DV + 1, :])
        o1 = acc1[:DIFF_DV, :] * (1.0 / acc1[DIFF_DV:DIFF_DV + 1, :])
        o = o0 - lam * o1
        ms = jnp.mean(o * o, axis=0, keepdims=True)
        o = o * lax.rsqrt(ms + NORM_EPS) * gn_ref[hs, :] * (1.0 - lam_init)
        o_ref[:, hs] = o.T.astype(BF16)


def _diff_attn(dq, dk, dvt, lam_vecs, diff_norm, lam_init, batch, seq):
    tq = TQ_ATT
    tk = dvt.shape[2]
    assert tq == tk
    nq = seq // tq
    hb = HB_ATT
    hv = DIFF_HEADS * DIFF_DV
    vec = pl.BlockSpec((1, DIFF_DQK), lambda b, h, i: (0, 0))
    return pl.pallas_call(
        functools.partial(_diff_body, lam_init),
        out_shape=jax.ShapeDtypeStruct((batch * seq, hv), BF16),
        grid=(batch, DIFF_HEADS // hb, nq),
        in_specs=[
            vec, vec, vec, vec,
            pl.BlockSpec((tq, hb * LANES), lambda b, h, i: (b * nq + i, h)),
            pl.BlockSpec((seq, hb * LANES), lambda b, h, i: (b, h)),
            pl.BlockSpec((seq // tk, hb * DIFF_DV, tk), lambda b, h, i: (b, h, 0)),
            pl.BlockSpec((hb * DIFF_DV, 1), lambda b, h, i: (h, 0)),
        ],
        out_specs=pl.BlockSpec((tq, hb * DIFF_DV), lambda b, h, i: (b * nq + i, h)),
        scratch_shapes=[
            pltpu.VMEM((2 * hb, 1, tq), F32),
            pltpu.VMEM((2 * hb, DIFF_DV + L_ROWS, tq), F32),
            pltpu.VMEM((2 * hb, tk, tq), BF16),
            pltpu.VMEM((2 * hb, 1, tq), F32),
        ],
        compiler_params=pltpu.CompilerParams(
            dimension_semantics=("arbitrary", "arbitrary", "arbitrary"),
            vmem_limit_bytes=VMEM_LIMIT_BYTES),
        name="diff_attn",
    )(*lam_vecs, dq, dk, dvt, diff_norm)


def _tail_body(tiles_per_seq, final, x_ref, og_ref, od_ref, p_ref, wo_ref, nf_ref, wu_ref,
               cw_ref, cb_ref, wd_ref, npl_ref, wpg_ref, wpp_ref, nfin_ref,
               y_ref, carry_ref, perm_ref, act_ref):
    tm, d = x_ref.shape
    c = FF_CHUNK
    n_chunks = wd_ref.shape[0] // c
    nseg = SUBLANES
    seg = tm // nseg
    nl = d // LANES

    @pl.when(pl.program_id(0) % tiles_per_seq == 0)
    def _():
        carry_ref[...] = jnp.zeros_like(carry_ref)

    half = og_ref.shape[1]
    h1 = (x_ref[...] + _dot(og_ref[...], wo_ref[0:half, :])
          + _dot(od_ref[...], wo_ref[half:2 * half, :]))

    def permute(a):
        for l in range(nl):
            for s in range(nseg):
                perm_ref[l, s * PERM_PITCH:s * PERM_PITCH + seg, :] = (
                    a[s * seg:(s + 1) * seg, l * LANES:(l + 1) * LANES])
        return jnp.concatenate(
            [jnp.concatenate([perm_ref[l, pl.ds(v, nseg, stride=PERM_PITCH), :] for v in range(seg)],
                             axis=0) for l in range(nl)], axis=1)

    def unpermute(a):
        for l in range(nl):
            for v in range(seg):
                perm_ref[l, pl.ds(v, nseg, stride=PERM_PITCH), :] = (
                    a[v * nseg:(v + 1) * nseg, l * LANES:(l + 1) * LANES])
        return jnp.concatenate(
            [jnp.concatenate([perm_ref[l, s * PERM_PITCH:s * PERM_PITCH + seg, :] for s in range(nseg)],
                             axis=0) for l in range(nl)], axis=1)

    u = permute(_rms(h1, nf_ref[...])).astype(BF16)

    d_ff = n_chunks * c

    def cols(ref, j):
        return jnp.concatenate([ref[:, j * c:(j + 1) * c], ref[:, d_ff + j * c:d_ff + (j + 1) * c]], axis=1)

    def up(j):
        return jnp.concatenate([_dot(u, wu_ref[:, j * c:(j + 1) * c]),
                                _dot(u, wu_ref[:, d_ff + j * c:d_ff + (j + 1) * c])], axis=1)

    row = lax.broadcasted_iota(jnp.int32, (SUBLANES, 2 * c), 0)

    def wrapped(last_block, prev_last_block):
        return pltpu.roll(jnp.where(row == SUBLANES - 1, prev_last_block, last_block), 1, 0)

    z_next = up(0)
    for j in range(n_chunks):
        z = z_next
        if j + 1 < n_chunks:
            z_next = up(j + 1)
        prev = carry_ref[j]
        carry_ref[j] = z[tm - 2 * SUBLANES:, :]
        b1 = wrapped(z[tm - SUBLANES:, :], prev[SUBLANES:, :])
        b2 = wrapped(z[tm - 2 * SUBLANES:tm - SUBLANES, :], prev[:SUBLANES, :])
        s1 = jnp.concatenate([b1, z[:tm - SUBLANES, :]], axis=0)
        s2 = jnp.concatenate([b2, b1, z[:tm - 2 * SUBLANES, :]], axis=0)
        cw = cols(cw_ref, j)
        cv = cols(cb_ref, j) + s2 * cw[0:1, :] + s1 * cw[1:2, :] + z * cw[2:3, :]
        gate = cv[:, :c]
        y2 = gate * (GELU_A + GELU_B * (gate * gate))
        act_ref[:, j * c:(j + 1) * c] = (gate / (1.0 + jnp.exp(-y2)) * cv[:, c:]).astype(BF16)

    h2 = h1 + unpermute(_dot(act_ref[...], wd_ref[...]))
    gate = jax.nn.sigmoid(_dot(_rms(h2, npl_ref[...]).astype(BF16), wpg_ref[...]))
    h3 = h2 + gate * _dot(p_ref[...].astype(BF16), wpp_ref[...])
    y_ref[...] = _rms(h3, nfin_ref[...]) if final else h3


def _tail(x2, o_gla, o_diff, p2, w_out, norm_ffn, wu, cw, cb, wd, norm_ple, w_pg, w_pp,
          norm_final, seq, final):
    n, d = x2.shape
    tm = TM_OUT
    row = lambda i: (i, 0)
    c2 = lambda i: (0, 0)
    c3 = lambda i: (0, 0, 0)
    one = pl.Buffered(1)

    def resident(a):
        return pl.BlockSpec(a.shape, c3 if a.ndim == 3 else c2, pipeline_mode=one)

    n_chunks = wd.shape[0] // FF_CHUNK
    return pl.pallas_call(
        functools.partial(_tail_body, seq // tm, final),
        out_shape=jax.ShapeDtypeStruct((n, d), F32),
        grid=(n // tm,),
        in_specs=[
            pl.BlockSpec((tm, d), row),
            pl.BlockSpec((tm, o_gla.shape[1]), row),
            pl.BlockSpec((tm, o_diff.shape[1]), row),
            pl.BlockSpec((tm, p2.shape[1]), row),
            resident(w_out), resident(norm_ffn), resident(wu), resident(cw),
            resident(cb), resident(wd), resident(norm_ple), resident(w_pg), resident(w_pp),
            resident(norm_final),
        ],
        out_specs=pl.BlockSpec((tm, d), row),
        scratch_shapes=[
            pltpu.VMEM((n_chunks, 2 * SUBLANES, 2 * FF_CHUNK), F32),
            pltpu.VMEM((d // LANES, SUBLANES * PERM_PITCH, LANES), F32),
            pltpu.VMEM((tm, n_chunks * FF_CHUNK), BF16),
        ],
        compiler_params=pltpu.CompilerParams(
            dimension_semantics=("arbitrary",), vmem_limit_bytes=VMEM_LIMIT_BYTES),
        name="tail",
    )(x2, o_gla, o_diff, p2, w_out, norm_ffn, wu, cw, cb, wd, norm_ple, w_pg, w_pp, norm_final)


def kernel(x, p, positions, norm_mix, w_in, w_a_up, b_a, gla_norm, lam_q1, lam_k1, lam_q2, lam_k2,
           diff_norm, w_out, norm_ffn, w_up, conv_w, conv_b, w_down, norm_ple, w_ple_gate,
           w_ple_proj, norm_final):
    batch, seq, d = x.shape
    depth = w_in.shape[0]
    n = batch * seq
    d_ff = w_down.shape[1]
    assert seq % max(TM_IN, TC_GLA, TQ_ATT, TM_OUT) == 0 and TM_IN % TK_ATT == 0
    assert d_ff % FF_CHUNK == 0 and TK_ATT % CHUNK == 0

    inv_freq = ROPE_THETA ** (-jnp.arange(0, DIFF_DQK, 2, dtype=F32) / DIFF_DQK)
    invf = jnp.tile(inv_freq, LANES // (DIFF_DQK // 2))[None, :]
    sgn = jnp.tile(jnp.concatenate([-jnp.ones((DIFF_DQK // 2,), F32),
                                    jnp.ones((DIFF_DQK // 2,), F32)]), LANES // DIFF_DQK)[None, :]

    h = x.reshape(n, d)
    pos2 = positions.reshape(n, 1)
    for i in range(depth):
        wi = w_in[i]
        ga0 = 1536
        dv0 = ga0 + GLA_GATE_RANK + 1024
        w_in_r = jnp.concatenate(
            [wi[:, :ga0], wi[:, ga0 + GLA_GATE_RANK:dv0], wi[:, ga0:ga0 + GLA_GATE_RANK],
             jnp.zeros((d, LANES - GLA_GATE_RANK), F32)], axis=1).astype(BF16)
        w_dvt = wi[:, dv0:].T.astype(BF16)
        wa_pad = jnp.concatenate(
            [w_a_up[i], jnp.zeros((LANES - GLA_GATE_RANK, w_a_up.shape[2]), F32)], axis=0).astype(BF16)

        gq, gk, gv, gg, la, dq, dk, dvt = _inproj(
            h, pos2, norm_mix[i][None, :], w_in_r, w_dvt, wa_pad, b_a[i][None, :], invf, sgn)
        o_gla = _gla(gq, gk, gv, gg, la, gla_norm[i][None, :], batch, seq)

        lam_init = 0.8 - 0.6 * math.exp(-0.3 * i)
        lam_vecs = [v[i][None, :] for v in (lam_q1, lam_k1, lam_q2, lam_k2)]
        o_diff = _diff_attn(dq, dk, dvt, lam_vecs, diff_norm[i][:, None], lam_init, batch, seq)

        h = _tail(h, o_gla, o_diff, p[i].reshape(n, -1), w_out[i].astype(BF16),
                  norm_ffn[i][None, :], w_up[i].astype(BF16), conv_w[i], conv_b[i][None, :],
                  w_down[i].astype(BF16), norm_ple[i][None, :],
                  w_ple_gate[i].astype(BF16), w_ple_proj[i].astype(BF16),
                  norm_final[None, :], seq, i == depth - 1)
    return h.reshape(batch, seq, d)
```

```python
import functools
import math

import jax
import jax.numpy as jnp
from jax import lax
from jax.experimental import pallas as pl
from jax.experimental.pallas import tpu as pltpu

F32 = jnp.float32
BF16 = jnp.bfloat16

CHUNK = 64
ROPE_THETA = 10000.0
NORM_EPS = 1e-6
GLA_HEADS = 4
GLA_DK = 64
GLA_DV = 128
GLA_GATE_RANK = 16
GLA_GATE_NORM = 16.0
DIFF_HEADS = 4
DIFF_DQK = 64
DIFF_DV = 128
CONV_W = 3
NEG_INF = -1e30
MASK_VALUE = -(2.0 ** 100)
PERM_PITCH = 72
L_ROWS = 16
LOG2E = 1.4426950408889634
GELU_A = 2.0 * math.sqrt(2.0 / math.pi)
GELU_B = GELU_A * 0.044715

LANES = 128
SUBLANES = 8
VMEM_LIMIT_BYTES = 56 * 1024 * 1024

TM_IN = 512
TC_GLA = 512
TQ_ATT = 256
TK_ATT = 256
HB_ATT = 4
TM_OUT = 512
FF_CHUNK = 256


def _rms(x, g):
    return x * lax.rsqrt(jnp.mean(x * x, axis=-1, keepdims=True) + NORM_EPS) * g


def _dot(a, b):
    return jnp.dot(a, b, preferred_element_type=F32)


def _dot_nt(a, b):
    return lax.dot_general(a, b, (((1,), (1,)), ((), ())), preferred_element_type=F32)


def _dot_tn(a, b):
    return lax.dot_general(a, b, (((0,), (0,)), ((), ())), preferred_element_type=F32)


def _inproj_body(x_ref, pos_ref, g_ref, w_ref, wvt_ref, wa_ref, ba_ref, invf_ref, sgn_ref,
                 gq_ref, gk_ref, gv_ref, gg_ref, la_ref, dq_ref, dk_ref, dvt_ref):
    tm = pos_ref.shape[0]
    nfreq = DIFF_DQK // 2
    groups = LANES // nfreq
    assert tm % (LANES * groups) == 0
    pos = pos_ref[...].astype(F32)
    lane_group = lax.broadcasted_iota(jnp.int32, (LANES, LANES), 1) // nfreq

    def spread(packed):
        blocks = []
        for jg in range(groups):
            x = jnp.where(lane_group == jg, packed, 0.0)
            x = x + pltpu.roll(x, 2 * nfreq, 1)
            blocks.append(x + pltpu.roll(x, nfreq, 1))
        return blocks

    cos_blocks, sin_blocks = [], []
    for r0 in range(0, tm, LANES * groups):
        pos_packed = pos[r0:r0 + LANES, :]
        for jg in range(1, groups):
            pos_packed = jnp.where(lane_group == jg,
                                   pos[r0 + jg * LANES:r0 + (jg + 1) * LANES, :], pos_packed)
        ang = pos_packed * invf_ref[...]
        cos_blocks += spread(jnp.cos(ang))
        sin_blocks += spread(jnp.sin(ang))
    cos = jnp.concatenate(cos_blocks, axis=0)
    sin = jnp.concatenate(sin_blocks, axis=0) * sgn_ref[...]
    lane = lax.broadcasted_iota(jnp.int32, (tm, LANES), 1)
    first_half = (lane & nfreq) == 0

    u = _rms(x_ref[...], g_ref[...]).astype(BF16)

    def proj(c0, c1):
        return _dot(u, w_ref[:, c0:c1])

    def rope(z, scale):
        outs = []
        for j in range(z.shape[1] // LANES):
            zj = z[:, j * LANES:(j + 1) * LANES]
            rot = jnp.where(first_half,
                            pltpu.roll(zj, LANES - DIFF_DQK // 2, 1),
                            pltpu.roll(zj, DIFF_DQK // 2, 1))
            outs.append(((zj * cos + rot * sin) * scale).astype(BF16))
        return jnp.concatenate(outs, axis=1)

    dq_ref[...] = rope(proj(1536, 2048), DIFF_DQK ** -0.5 * LOG2E)
    dk_ref[...] = rope(proj(2048, 2560), 1.0)

    a_low = proj(2560, 2688).astype(BF16)
    pre = _dot(a_low, wa_ref[...]) + ba_ref[...]
    log_sig = jnp.minimum(pre, 0.0) - jnp.log1p(jnp.exp(-jnp.abs(pre)))
    la_ref[...] = log_sig * (1.0 / GLA_GATE_NORM)

    gq_ref[...] = (proj(0, 256) * (GLA_DK ** -0.5)).astype(BF16)
    gk_ref[...] = proj(256, 512).astype(BF16)
    gv_ref[...] = proj(512, 1024).astype(BF16)
    gg_ref[...] = proj(1024, 1536).astype(BF16)
    dvt = _dot_nt(wvt_ref[...], u)
    tk = dvt_ref.shape[2]
    for c in range(dvt_ref.shape[0]):
        dvt_ref[c] = dvt[:, c * tk:(c + 1) * tk].astype(BF16)


def _inproj(x2, pos2, norm_mix, w_in_r, w_dvt, wa_pad, b_a, invf, sgn):
    n, d = x2.shape
    tm = TM_IN
    row = lambda i: (i, 0)
    const = lambda i: (0, 0)
    widths = (256, 256, 512, 512, 256, 512, 512)
    dtypes = (BF16, BF16, BF16, BF16, F32, BF16, BF16)
    tk = TK_ATT
    hv = w_dvt.shape[0]
    return pl.pallas_call(
        _inproj_body,
        out_shape=[jax.ShapeDtypeStruct((n, w), dt) for w, dt in zip(widths, dtypes)]
        + [jax.ShapeDtypeStruct((n // tk, hv, tk), BF16)],
        grid=(n // tm,),
        in_specs=[
            pl.BlockSpec((tm, d), row),
            pl.BlockSpec((tm, 1), row),
            pl.BlockSpec((1, d), const),
            pl.BlockSpec(w_in_r.shape, const),
            pl.BlockSpec(w_dvt.shape, const),
            pl.BlockSpec(wa_pad.shape, const),
            pl.BlockSpec((1, 256), const),
            pl.BlockSpec((1, LANES), const),
            pl.BlockSpec((1, LANES), const),
        ],
        out_specs=[pl.BlockSpec((tm, w), row) for w in widths]
        + [pl.BlockSpec((tm // tk, hv, tk), lambda i: (i, 0, 0))],
        compiler_params=pltpu.CompilerParams(
            dimension_semantics=("arbitrary",), vmem_limit_bytes=VMEM_LIMIT_BYTES),
        name="inproj",
    )(x2, pos2, norm_mix, w_in_r, w_dvt, wa_pad, b_a, invf, sgn)


def _gla_body(q_ref, k_ref, v_ref, g_ref, la_ref, gn_ref, o_ref, st_ref):
    @pl.when(pl.program_id(1) == 0)
    def _():
        st_ref[...] = jnp.zeros_like(st_ref)

    hk = GLA_HEADS * GLA_DK
    r_i = lax.broadcasted_iota(jnp.int32, (CHUNK, CHUNK), 0)
    c_i = lax.broadcasted_iota(jnp.int32, (CHUNK, CHUNK), 1)
    tril = (r_i >= c_i).astype(BF16)
    lane_head = lax.broadcasted_iota(jnp.int32, (1, hk), 1) // GLA_DK
    head_mask = [lane_head == h for h in range(GLA_HEADS)]
    rr = lax.broadcasted_iota(jnp.int32, (GLA_HEADS * CHUNK, CHUNK), 0) % CHUNK
    cc = lax.broadcasted_iota(jnp.int32, (GLA_HEADS * CHUNK, CHUNK), 1)
    tri_stack = rr >= cc
    gn = gn_ref[...]

    n_chunks = q_ref.shape[0] // CHUNK
    chunks = range(n_chunks)
    rows_of = [slice(c * CHUNK, (c + 1) * CHUNK) for c in chunks]
    head_rows = [slice(h * CHUNK, (h + 1) * CHUNK) for h in range(GLA_HEADS)]
    head_keys = [slice(h * GLA_DK, (h + 1) * GLA_DK) for h in range(GLA_HEADS)]
    head_vals = [slice(h * GLA_DV, (h + 1) * GLA_DV) for h in range(GLA_HEADS)]

    def cumsum(c):
        la = la_ref[rows_of[c], :]
        la_hi = la.astype(BF16)
        la_lo = (la - la_hi.astype(F32)).astype(BF16)
        return _dot(tril, la_hi) + _dot(tril, la_lo)

    bs = [cumsum(c) for c in chunks]

    def gated(c):
        b = bs[c]
        eb = jnp.exp(b)
        enb = jnp.exp(-b)
        q = q_ref[rows_of[c], :].astype(F32)
        k = k_ref[rows_of[c], :].astype(F32)
        qf = q * eb
        qb = q * enb
        qf_st = jnp.concatenate([jnp.where(m, qf, 0.0) for m in head_mask], axis=0).astype(BF16)
        qb_st = jnp.concatenate([jnp.where(m, qb, 0.0) for m in head_mask], axis=0).astype(BF16)
        k_dec = (k * jnp.exp(b[CHUNK - 1:CHUNK, :] - b)).astype(BF16)
        decay = jnp.exp(b[CHUNK - SUBLANES:, :].T[:, SUBLANES - 1:SUBLANES])
        return qf_st, qb_st, (k * enb).astype(BF16), (k * eb).astype(BF16), k_dec.T, decay

    gs = [gated(c) for c in chunks]
    a_fwd = [_dot_nt(gs[c][0], gs[c][2]) for c in chunks]
    a_bwd = [_dot_nt(gs[c][1], gs[c][3]) for c in chunks]
    a = [jnp.where(tri_stack, a_fwd[c], a_bwd[c]).astype(BF16) for c in chunks]
    o_intra = [[_dot(a[c][head_rows[h], :], v_ref[rows_of[c], head_vals[h]])
                for h in range(GLA_HEADS)] for c in chunks]
    d_st = [jnp.concatenate([_dot(gs[c][4][head_keys[h], :], v_ref[rows_of[c], head_vals[h]])
                             for h in range(GLA_HEADS)], axis=0) for c in chunks]

    st = st_ref[...]
    states = []
    for c in chunks:
        states.append(st.astype(BF16))
        st = st * gs[c][5] + d_st[c]
    st_ref[...] = st

    o_inter = [_dot(gs[c][0], states[c]) for c in chunks]
    for c in chunks:
        rows = rows_of[c]
        for h in range(GLA_HEADS):
            vs = head_vals[h]
            o = o_intra[c][h] + o_inter[c][head_rows[h], :]
            o = _rms(o, gn[:, vs])
            g = g_ref[rows, vs].astype(F32)
            o_ref[rows, vs] = (o * (g * jax.nn.sigmoid(g))).astype(BF16)


def _gla(gq, gk, gv, gg, la, gla_norm, batch, seq):
    tc = TC_GLA
    nt = seq // tc
    row = lambda b, t: (b * nt + t, 0)
    const = lambda b, t: (0, 0)
    hk = GLA_HEADS * GLA_DK
    hv = GLA_HEADS * GLA_DV
    return pl.pallas_call(
        _gla_body,
        out_shape=jax.ShapeDtypeStruct((batch * seq, hv), BF16),
        grid=(batch, nt),
        in_specs=[
            pl.BlockSpec((tc, hk), row),
            pl.BlockSpec((tc, hk), row),
            pl.BlockSpec((tc, hv), row),
            pl.BlockSpec((tc, hv), row),
            pl.BlockSpec((tc, hk), row),
            pl.BlockSpec((1, hv), const),
        ],
        out_specs=pl.BlockSpec((tc, hv), row),
        scratch_shapes=[pltpu.VMEM((hk, GLA_DV), F32)],
        compiler_params=pltpu.CompilerParams(
            dimension_semantics=("arbitrary", "arbitrary"), vmem_limit_bytes=VMEM_LIMIT_BYTES),
        name="gla",
    )(gq, gk, gv, gg, la, gla_norm)


def _diff_body(lam_init, lq1_ref, lk1_ref, lq2_ref, lk2_ref, q_ref, qn_ref, k_ref, vt_ref, gn_ref,
               o_ref, m_ref, acc_ref, s_ref, mb_ref):
    tq = q_ref.shape[0]
    tk = vt_ref.shape[2]
    hb = q_ref.shape[1] // LANES
    qi = pl.program_id(2)
    nq = pl.num_programs(2)

    lane = lax.broadcasted_iota(jnp.int32, (tq, LANES), 1)

    def chains(ref):
        out = []
        for a in range(hb):
            q = ref[:, a * LANES:(a + 1) * LANES]
            zero = jnp.zeros_like(q)
            out.append(jnp.where(lane < DIFF_DQK, q, zero))
            out.append(jnp.where(lane >= DIFF_DQK, q, zero))
        return out

    qs = chains(q_ref)
    nc = 2 * hb

    heads = [slice((c // 2) * LANES, (c // 2 + 1) * LANES) for c in range(nc)]

    m_ref[...] = jnp.full_like(m_ref, MASK_VALUE)

    @pl.when((pl.program_id(0) == 0) & (pl.program_id(1) == 0) & (qi == 0))
    def _():
        acc_ref[...] = jnp.zeros_like(acc_ref)

    ones_rows = jnp.ones((L_ROWS, tk), BF16)

    def scores(t, c, masked, q_chains=qs):
        k0 = pl.multiple_of(t * tk, tk)
        s_new = _dot_nt(k_ref[pl.ds(k0, tk), heads[c]], q_chains[c])
        if masked:
            k_chunk = t * (tk // CHUNK) + lax.broadcasted_iota(jnp.int32, (tk, tq), 0) // CHUNK
            q_chunk = qi * (tq // CHUNK) + lax.broadcasted_iota(jnp.int32, (tk, tq), 1) // CHUNK
            s_new = jnp.where(k_chunk <= q_chunk, s_new, MASK_VALUE)
        return s_new

    def keep_scores(c, s_new):
        sb = s_new.astype(BF16)
        s_ref[c] = sb
        mb_ref[c] = jnp.max(sb, axis=0, keepdims=True).astype(F32)

    def softmax_pv(t, c):
        m_prev = m_ref[c]
        m_next = jnp.maximum(m_prev, mb_ref[c])
        alpha = jnp.exp2(m_prev - m_next)
        p = jnp.exp2(s_ref[c] - m_next.astype(BF16))
        m_ref[c] = m_next
        v1 = jnp.concatenate([vt_ref[t, heads[c], :], ones_rows], axis=0)
        return alpha, _dot(v1, p)

    def both(t, masked):
        for c in range(nc):
            alpha, pv = softmax_pv(t - 1, c)
            s_new = scores(t, c, masked)
            acc_ref[c] = alpha * acc_ref[c] + pv
            keep_scores(c, s_new)

    @pl.when(qi == 0)
    def _():
        for c in range(nc):
            keep_scores(c, scores(0, c, True))

    @pl.when(qi > 0)
    def _():
        def pair(i, carry):
            both(2 * i + 1, False)
            both(2 * i + 2, False)
            return carry
        n_pairs = (qi - 1) // 2
        lax.fori_loop(0, n_pairs, pair, 0)

        @pl.when((qi - 1) % 2 == 1)
        def _():
            both(qi - 1, False)

        both(qi, True)

    lam = (jnp.exp(jnp.sum(lq1_ref[...] * lk1_ref[...], axis=1, keepdims=True))
           - jnp.exp(jnp.sum(lq2_ref[...] * lk2_ref[...], axis=1, keepdims=True))
           + lam_init)

    def last_stage(q_next):
        for a in range(hb):
            acc = []
            for c in (2 * a, 2 * a + 1):
                alpha, pv = softmax_pv(qi, c)
                if q_next is not None:
                    s_new = scores(0, c, False, q_next)
                acc.append(alpha * acc_ref[c] + pv)
                if q_next is not None:
                    keep_scores(c, s_new)
            hs = slice(a * LANES, (a + 1) * LANES)
            o0 = acc[0][:DIFF_DV, :] * (1.0 / acc[0][DIFF_DV:DIFF_DV + 1, :])
            o1 = acc[1][:DIFF_DV, :] * (1.0 / acc[1][DIFF_DV:DIFF_DV + 1, :])
            o = o0 - lam * o1
            ms = jnp.mean(o * o, axis=0, keepdims=True)
            o = o * lax.rsqrt(ms + NORM_EPS) * gn_ref[hs, :] * (1.0 - lam_init)
            o_ref[:, hs] = o.T.astype(BF16)

    @pl.when(qi < nq - 1)
    def _():
        last_stage(chains(qn_ref))

    @pl.when(qi == nq - 1)
    def _():
        last_stage(None)


def _diff_attn(dq, dk, dvt, lam_vecs, diff_norm, lam_init, batch, seq):
    tq = TQ_ATT
    tk = dvt.shape[2]
    assert tq == tk
    nq = seq // tq
    hb = HB_ATT
    hv = DIFF_HEADS * DIFF_DV
    vec = pl.BlockSpec((1, DIFF_DQK), lambda b, h, i: (0, 0))
    return pl.pallas_call(
        functools.partial(_diff_body, lam_init),
        out_shape=jax.ShapeDtypeStruct((batch * seq, hv), BF16),
        grid=(batch, DIFF_HEADS // hb, nq),
        in_specs=[
            vec, vec, vec, vec,
            pl.BlockSpec((tq, hb * LANES), lambda b, h, i: (b * nq + i, h)),
            pl.BlockSpec((tq, hb * LANES), lambda b, h, i: (b * nq + jnp.minimum(i + 1, nq - 1), h)),
            pl.BlockSpec((seq, hb * LANES), lambda b, h, i: (b, h)),
            pl.BlockSpec((seq // tk, hb * DIFF_DV, tk), lambda b, h, i: (b, h, 0)),
            pl.BlockSpec((hb * DIFF_DV, 1), lambda b, h, i: (h, 0)),
        ],
        out_specs=pl.BlockSpec((tq, hb * DIFF_DV), lambda b, h, i: (b * nq + i, h)),
        scratch_shapes=[
            pltpu.VMEM((2 * hb, 1, tq), F32),
            pltpu.VMEM((2 * hb, DIFF_DV + L_ROWS, tq), F32),
            pltpu.VMEM((2 * hb, tk, tq), BF16),
            pltpu.VMEM((2 * hb, 1, tq), F32),
        ],
        compiler_params=pltpu.CompilerParams(
            dimension_semantics=("arbitrary", "arbitrary", "arbitrary"),
            vmem_limit_bytes=VMEM_LIMIT_BYTES),
        name="diff_attn",
    )(*lam_vecs, dq, dq, dk, dvt, diff_norm)


def _tail_body(tiles_per_seq, final, x_ref, og_ref, od_ref, p_ref, wo_ref, nf_ref, wu_ref,
               cw_ref, cb_ref, wd_ref, npl_ref, wpg_ref, wpp_ref, nfin_ref,
               y_ref, carry_ref, perm_ref, act_ref):
    tm, d = x_ref.shape
    c = FF_CHUNK
    n_chunks = wd_ref.shape[0] // c
    nseg = SUBLANES
    seg = tm // nseg
    nl = d // LANES

    @pl.when(pl.program_id(0) % tiles_per_seq == 0)
    def _():
        carry_ref[...] = jnp.zeros_like(carry_ref)

    half = og_ref.shape[1]
    h1 = (x_ref[...] + _dot(og_ref[...], wo_ref[0:half, :])
          + _dot(od_ref[...], wo_ref[half:2 * half, :]))
    ple = _dot(p_ref[...].astype(BF16), wpp_ref[...])

    def permute(a):
        for l in range(nl):
            for s in range(nseg):
                perm_ref[l, s * PERM_PITCH:s * PERM_PITCH + seg, :] = (
                    a[s * seg:(s + 1) * seg, l * LANES:(l + 1) * LANES])
        return jnp.concatenate(
            [jnp.concatenate([perm_ref[l, pl.ds(v, nseg, stride=PERM_PITCH), :] for v in range(seg)],
                             axis=0) for l in range(nl)], axis=1)

    def unpermute(a):
        for l in range(nl):
            for v in range(seg):
                perm_ref[l, pl.ds(v, nseg, stride=PERM_PITCH), :] = (
                    a[v * nseg:(v + 1) * nseg, l * LANES:(l + 1) * LANES])
        return jnp.concatenate(
            [jnp.concatenate([perm_ref[l, s * PERM_PITCH:s * PERM_PITCH + seg, :] for s in range(nseg)],
                             axis=0) for l in range(nl)], axis=1)

    u = permute(_rms(h1, nf_ref[...])).astype(BF16)

    d_ff = n_chunks * c

    def cols(ref, j):
        return jnp.concatenate([ref[:, j * c:(j + 1) * c], ref[:, d_ff + j * c:d_ff + (j + 1) * c]], axis=1)

    def up(j):
        return jnp.concatenate([_dot(u, wu_ref[:, j * c:(j + 1) * c]),
                                _dot(u, wu_ref[:, d_ff + j * c:d_ff + (j + 1) * c])], axis=1)

    row = lax.broadcasted_iota(jnp.int32, (SUBLANES, 2 * c), 0)

    def wrapped(last_block, prev_last_block):
        return pltpu.roll(jnp.where(row == SUBLANES - 1, prev_last_block, last_block), 1, 0)

    z_next = up(0)
    for j in range(n_chunks):
        z = z_next
        if j + 1 < n_chunks:
            z_next = up(j + 1)
        prev = carry_ref[j]
        carry_ref[j] = z[tm - 2 * SUBLANES:, :]
        b1 = wrapped(z[tm - SUBLANES:, :], prev[SUBLANES:, :])
        b2 = wrapped(z[tm - 2 * SUBLANES:tm - SUBLANES, :], prev[:SUBLANES, :])
        s1 = jnp.concatenate([b1, z[:tm - SUBLANES, :]], axis=0)
        s2 = jnp.concatenate([b2, b1, z[:tm - 2 * SUBLANES, :]], axis=0)
        cw = cols(cw_ref, j)
        cv = cols(cb_ref, j) + s2 * cw[0:1, :] + s1 * cw[1:2, :] + z * cw[2:3, :]
        gate = cv[:, :c]
        y2 = gate * (GELU_A + GELU_B * (gate * gate))
        act_ref[:, j * c:(j + 1) * c] = (gate / (1.0 + jnp.exp(-y2)) * cv[:, c:]).astype(BF16)

    h2 = h1 + unpermute(_dot(act_ref[...], wd_ref[...]))
    gate = jax.nn.sigmoid(_dot(_rms(h2, npl_ref[...]).astype(BF16), wpg_ref[...]))
    h3 = h2 + gate * ple
    y_ref[...] = _rms(h3, nfin_ref[...]) if final else h3


def _tail(x2, o_gla, o_diff, p2, w_out, norm_ffn, wu, cw, cb, wd, norm_ple, w_pg, w_pp,
          norm_final, seq, final):
    n, d = x2.shape
    tm = TM_OUT
    row = lambda i: (i, 0)
    c2 = lambda i: (0, 0)
    c3 = lambda i: (0, 0, 0)
    one = pl.Buffered(1)

    def resident(a):
        return pl.BlockSpec(a.shape, c3 if a.ndim == 3 else c2, pipeline_mode=one)

    n_chunks = wd.shape[0] // FF_CHUNK
    return pl.pallas_call(
        functools.partial(_tail_body, seq // tm, final),
        out_shape=jax.ShapeDtypeStruct((n, d), F32),
        grid=(n // tm,),
        in_specs=[
            pl.BlockSpec((tm, d), row),
            pl.BlockSpec((tm, o_gla.shape[1]), row),
            pl.BlockSpec((tm, o_diff.shape[1]), row),
            pl.BlockSpec((tm, p2.shape[1]), row),
            resident(w_out), resident(norm_ffn), resident(wu), resident(cw),
            resident(cb), resident(wd), resident(norm_ple), resident(w_pg), resident(w_pp),
            resident(norm_final),
        ],
        out_specs=pl.BlockSpec((tm, d), row),
        scratch_shapes=[
            pltpu.VMEM((n_chunks, 2 * SUBLANES, 2 * FF_CHUNK), F32),
            pltpu.VMEM((d // LANES, SUBLANES * PERM_PITCH, LANES), F32),
            pltpu.VMEM((tm, n_chunks * FF_CHUNK), BF16),
        ],
        compiler_params=pltpu.CompilerParams(
            dimension_semantics=("arbitrary",), vmem_limit_bytes=VMEM_LIMIT_BYTES),
        name="tail",
    )(x2, o_gla, o_diff, p2, w_out, norm_ffn, wu, cw, cb, wd, norm_ple, w_pg, w_pp, norm_final)


def kernel(x, p, positions, norm_mix, w_in, w_a_up, b_a, gla_norm, lam_q1, lam_k1, lam_q2, lam_k2,
           diff_norm, w_out, norm_ffn, w_up, conv_w, conv_b, w_down, norm_ple, w_ple_gate,
           w_ple_proj, norm_final):
    batch, seq, d = x.shape
    depth = w_in.shape[0]
    n = batch * seq
    d_ff = w_down.shape[1]
    assert seq % max(TM_IN, TC_GLA, TQ_ATT, TM_OUT) == 0 and TM_IN % TK_ATT == 0
    assert d_ff % FF_CHUNK == 0 and TK_ATT % CHUNK == 0

    inv_freq = ROPE_THETA ** (-jnp.arange(0, DIFF_DQK, 2, dtype=F32) / DIFF_DQK)
    invf = jnp.tile(inv_freq, LANES // (DIFF_DQK // 2))[None, :]
    sgn = jnp.tile(jnp.concatenate([-jnp.ones((DIFF_DQK // 2,), F32),
                                    jnp.ones((DIFF_DQK // 2,), F32)]), LANES // DIFF_DQK)[None, :]

    h = x.reshape(n, d)
    pos2 = positions.reshape(n, 1)
    for i in range(depth):
        wi = w_in[i]
        ga0 = 1536
        dv0 = ga0 + GLA_GATE_RANK + 1024
        w_in_r = jnp.concatenate(
            [wi[:, :ga0], wi[:, ga0 + GLA_GATE_RANK:dv0], wi[:, ga0:ga0 + GLA_GATE_RANK],
             jnp.zeros((d, LANES - GLA_GATE_RANK), F32)], axis=1).astype(BF16)
        w_dvt = wi[:, dv0:].T.astype(BF16)
        wa_pad = jnp.concatenate(
            [w_a_up[i], jnp.zeros((LANES - GLA_GATE_RANK, w_a_up.shape[2]), F32)], axis=0).astype(BF16)

        gq, gk, gv, gg, la, dq, dk, dvt = _inproj(
            h, pos2, norm_mix[i][None, :], w_in_r, w_dvt, wa_pad, b_a[i][None, :], invf, sgn)
        o_gla = _gla(gq, gk, gv, gg, la, gla_norm[i][None, :], batch, seq)

        lam_init = 0.8 - 0.6 * math.exp(-0.3 * i)
        lam_vecs = [v[i][None, :] for v in (lam_q1, lam_k1, lam_q2, lam_k2)]
        o_diff = _diff_attn(dq, dk, dvt, lam_vecs, diff_norm[i][:, None], lam_init, batch, seq)

        h = _tail(h, o_gla, o_diff, p[i].reshape(n, -1), w_out[i].astype(BF16),
                  norm_ffn[i][None, :], w_up[i].astype(BF16), conv_w[i], conv_b[i][None, :],
                  w_down[i].astype(BF16), norm_ple[i][None, :],
                  w_ple_gate[i].astype(BF16), w_ple_proj[i].astype(BF16),
                  norm_final[None, :], seq, i == depth - 1)
    return h.reshape(batch, seq, d)
```

```python
import functools
import math

import jax
import jax.numpy as jnp
from jax import lax
from jax.experimental import pallas as pl
from jax.experimental.pallas import tpu as pltpu

F32 = jnp.float32
BF16 = jnp.bfloat16

CHUNK = 64
ROPE_THETA = 10000.0
NORM_EPS = 1e-6
GLA_HEADS = 4
GLA_DK = 64
GLA_DV = 128
GLA_GATE_RANK = 16
GLA_GATE_NORM = 16.0
DIFF_HEADS = 4
DIFF_DQK = 64
DIFF_DV = 128
CONV_W = 3
NEG_INF = -1e30
MASK_VALUE = -(2.0 ** 100)
PERM_PITCH = 72
L_ROWS = 16
LOG2E = 1.4426950408889634
GELU_A = 2.0 * math.sqrt(2.0 / math.pi)
GELU_B = GELU_A * 0.044715

LANES = 128
SUBLANES = 8
VMEM_LIMIT_BYTES = 56 * 1024 * 1024

TM_IN = 512
TC_GLA = 512
TQ_ATT = 256
TK_ATT = 256
HB_ATT = 4
TM_OUT = 512
FF_CHUNK = 256


def _rms(x, g):
    return x * lax.rsqrt(jnp.mean(x * x, axis=-1, keepdims=True) + NORM_EPS) * g


def _dot(a, b):
    return jnp.dot(a, b, preferred_element_type=F32)


def _dot_nt(a, b):
    return lax.dot_general(a, b, (((1,), (1,)), ((), ())), preferred_element_type=F32)


def _dot_tn(a, b):
    return lax.dot_general(a, b, (((0,), (0,)), ((), ())), preferred_element_type=F32)


def _inproj_body(x_ref, pos_ref, g_ref, w_ref, wvt_ref, wa_ref, ba_ref, invf_ref, sgn_ref,
                 gq_ref, gk_ref, gv_ref, gg_ref, la_ref, dq_ref, dk_ref, dvt_ref):
    tm = pos_ref.shape[0]
    nfreq = DIFF_DQK // 2
    groups = LANES // nfreq
    assert tm % (LANES * groups) == 0
    pos = pos_ref[...].astype(F32)
    lane_group = lax.broadcasted_iota(jnp.int32, (LANES, LANES), 1) // nfreq

    def spread(packed):
        blocks = []
        for jg in range(groups):
            x = jnp.where(lane_group == jg, packed, 0.0)
            x = x + pltpu.roll(x, 2 * nfreq, 1)
            blocks.append(x + pltpu.roll(x, nfreq, 1))
        return blocks

    cos_blocks, sin_blocks = [], []
    for r0 in range(0, tm, LANES * groups):
        pos_packed = pos[r0:r0 + LANES, :]
        for jg in range(1, groups):
            pos_packed = jnp.where(lane_group == jg,
                                   pos[r0 + jg * LANES:r0 + (jg + 1) * LANES, :], pos_packed)
        ang = pos_packed * invf_ref[...]
        cos_blocks += spread(jnp.cos(ang))
        sin_blocks += spread(jnp.sin(ang))
    cos = jnp.concatenate(cos_blocks, axis=0)
    sin = jnp.concatenate(sin_blocks, axis=0) * sgn_ref[...]
    lane = lax.broadcasted_iota(jnp.int32, (tm, LANES), 1)
    first_half = (lane & nfreq) == 0

    u = _rms(x_ref[...], g_ref[...]).astype(BF16)

    def proj(c0, c1):
        return _dot(u, w_ref[:, c0:c1])

    def rope(z, scale):
        outs = []
        for j in range(z.shape[1] // LANES):
            zj = z[:, j * LANES:(j + 1) * LANES]
            rot = jnp.where(first_half,
                            pltpu.roll(zj, LANES - DIFF_DQK // 2, 1),
                            pltpu.roll(zj, DIFF_DQK // 2, 1))
            outs.append(((zj * cos + rot * sin) * scale).astype(BF16))
        return jnp.concatenate(outs, axis=1)

    gq_ref[...] = (proj(0, 256) * (GLA_DK ** -0.5)).astype(BF16)
    gk_ref[...] = proj(256, 512).astype(BF16)

    dq_ref[...] = rope(proj(1536, 2048), DIFF_DQK ** -0.5 * LOG2E)
    dk_ref[...] = rope(proj(2048, 2560), 1.0)

    a_low = proj(2560, 2688).astype(BF16)
    pre = _dot(a_low, wa_ref[...]) + ba_ref[...]
    log_sig = jnp.minimum(pre, 0.0) - jnp.log1p(jnp.exp(-jnp.abs(pre)))
    la_ref[...] = log_sig * (1.0 / GLA_GATE_NORM)

    gv_ref[...] = proj(512, 1024).astype(BF16)
    gg_ref[...] = proj(1024, 1536).astype(BF16)
    dvt = _dot_nt(wvt_ref[...], u)
    tk = dvt_ref.shape[2]
    for c in range(dvt_ref.shape[0]):
        dvt_ref[c] = dvt[:, c * tk:(c + 1) * tk].astype(BF16)


def _inproj(x2, pos2, norm_mix, w_in_r, w_dvt, wa_pad, b_a, invf, sgn):
    n, d = x2.shape
    tm = TM_IN
    row = lambda i: (i, 0)
    const = lambda i: (0, 0)
    widths = (256, 256, 512, 512, 256, 512, 512)
    dtypes = (BF16, BF16, BF16, BF16, F32, BF16, BF16)
    tk = TK_ATT
    hv = w_dvt.shape[0]
    return pl.pallas_call(
        _inproj_body,
        out_shape=[jax.ShapeDtypeStruct((n, w), dt) for w, dt in zip(widths, dtypes)]
        + [jax.ShapeDtypeStruct((n // tk, hv, tk), BF16)],
        grid=(n // tm,),
        in_specs=[
            pl.BlockSpec((tm, d), row),
            pl.BlockSpec((tm, 1), row),
            pl.BlockSpec((1, d), const),
            pl.BlockSpec(w_in_r.shape, const),
            pl.BlockSpec(w_dvt.shape, const),
            pl.BlockSpec(wa_pad.shape, const),
            pl.BlockSpec((1, 256), const),
            pl.BlockSpec((1, LANES), const),
            pl.BlockSpec((1, LANES), const),
        ],
        out_specs=[pl.BlockSpec((tm, w), row) for w in widths]
        + [pl.BlockSpec((tm // tk, hv, tk), lambda i: (i, 0, 0))],
        compiler_params=pltpu.CompilerParams(
            dimension_semantics=("arbitrary",), vmem_limit_bytes=VMEM_LIMIT_BYTES),
        name="inproj",
    )(x2, pos2, norm_mix, w_in_r, w_dvt, wa_pad, b_a, invf, sgn)


def _gla_body(q_ref, k_ref, v_ref, g_ref, la_ref, gn_ref, o_ref, st_ref):
    @pl.when(pl.program_id(1) == 0)
    def _():
        st_ref[...] = jnp.zeros_like(st_ref)

    hk = GLA_HEADS * GLA_DK
    r_i = lax.broadcasted_iota(jnp.int32, (CHUNK, CHUNK), 0)
    c_i = lax.broadcasted_iota(jnp.int32, (CHUNK, CHUNK), 1)
    tril = (r_i >= c_i).astype(BF16)
    lane_head = lax.broadcasted_iota(jnp.int32, (1, hk), 1) // GLA_DK
    head_mask = [lane_head == h for h in range(GLA_HEADS)]
    rr = lax.broadcasted_iota(jnp.int32, (GLA_HEADS * CHUNK, CHUNK), 0) % CHUNK
    cc = lax.broadcasted_iota(jnp.int32, (GLA_HEADS * CHUNK, CHUNK), 1)
    tri_stack = rr >= cc
    gn = gn_ref[...]

    n_chunks = q_ref.shape[0] // CHUNK
    chunks = range(n_chunks)
    rows_of = [slice(c * CHUNK, (c + 1) * CHUNK) for c in chunks]
    head_rows = [slice(h * CHUNK, (h + 1) * CHUNK) for h in range(GLA_HEADS)]
    head_keys = [slice(h * GLA_DK, (h + 1) * GLA_DK) for h in range(GLA_HEADS)]
    head_vals = [slice(h * GLA_DV, (h + 1) * GLA_DV) for h in range(GLA_HEADS)]

    def cumsum(c):
        la = la_ref[rows_of[c], :]
        la_hi = la.astype(BF16)
        la_lo = (la - la_hi.astype(F32)).astype(BF16)
        return _dot(tril, la_hi) + _dot(tril, la_lo)

    bs = [cumsum(c) for c in chunks]

    def gated(c):
        b = bs[c]
        eb = jnp.exp(b)
        enb = jnp.exp(-b)
        q = q_ref[rows_of[c], :].astype(F32)
        k = k_ref[rows_of[c], :].astype(F32)
        qf = q * eb
        qb = q * enb
        qf_st = jnp.concatenate([jnp.where(m, qf, 0.0) for m in head_mask], axis=0).astype(BF16)
        qb_st = jnp.concatenate([jnp.where(m, qb, 0.0) for m in head_mask], axis=0).astype(BF16)
        k_dec = (k * jnp.exp(b[CHUNK - 1:CHUNK, :] - b)).astype(BF16)
        decay = jnp.exp(b[CHUNK - SUBLANES:, :].T[:, SUBLANES - 1:SUBLANES])
        return qf_st, qb_st, (k * enb).astype(BF16), (k * eb).astype(BF16), k_dec.T, decay

    gs = [gated(c) for c in chunks]
    a_fwd = [_dot_nt(gs[c][0], gs[c][2]) for c in chunks]
    a_bwd = [_dot_nt(gs[c][1], gs[c][3]) for c in chunks]
    a = [jnp.where(tri_stack, a_fwd[c], a_bwd[c]).astype(BF16) for c in chunks]
    o_intra = [[_dot(a[c][head_rows[h], :], v_ref[rows_of[c], head_vals[h]])
                for h in range(GLA_HEADS)] for c in chunks]
    d_st = [jnp.concatenate([_dot(gs[c][4][head_keys[h], :], v_ref[rows_of[c], head_vals[h]])
                             for h in range(GLA_HEADS)], axis=0) for c in chunks]

    st = st_ref[...]
    states = []
    for c in chunks:
        states.append(st.astype(BF16))
        st = st * gs[c][5] + d_st[c]
    st_ref[...] = st

    o_inter = [_dot(gs[c][0], states[c]) for c in chunks]
    for c in chunks:
        rows = rows_of[c]
        for h in range(GLA_HEADS):
            vs = head_vals[h]
            o = o_intra[c][h] + o_inter[c][head_rows[h], :]
            o = _rms(o, gn[:, vs])
            g = g_ref[rows, vs].astype(F32)
            o_ref[rows, vs] = (o * (g * jax.nn.sigmoid(g))).astype(BF16)


def _gla(gq, gk, gv, gg, la, gla_norm, batch, seq):
    tc = TC_GLA
    nt = seq // tc
    row = lambda b, t: (b * nt + t, 0)
    const = lambda b, t: (0, 0)
    hk = GLA_HEADS * GLA_DK
    hv = GLA_HEADS * GLA_DV
    return pl.pallas_call(
        _gla_body,
        out_shape=jax.ShapeDtypeStruct((batch * seq, hv), BF16),
        grid=(batch, nt),
        in_specs=[
            pl.BlockSpec((tc, hk), row),
            pl.BlockSpec((tc, hk), row),
            pl.BlockSpec((tc, hv), row),
            pl.BlockSpec((tc, hv), row),
            pl.BlockSpec((tc, hk), row),
            pl.BlockSpec((1, hv), const),
        ],
        out_specs=pl.BlockSpec((tc, hv), row),
        scratch_shapes=[pltpu.VMEM((hk, GLA_DV), F32)],
        compiler_params=pltpu.CompilerParams(
            dimension_semantics=("arbitrary", "arbitrary"), vmem_limit_bytes=VMEM_LIMIT_BYTES),
        name="gla",
    )(gq, gk, gv, gg, la, gla_norm)


def _diff_body(lam_init, lq1_ref, lk1_ref, lq2_ref, lk2_ref, q_ref, qn_ref, k_ref, vt_ref, gn_ref,
               o_ref, m_ref, acc_ref, s_ref, mb_ref):
    tq = q_ref.shape[0]
    tk = vt_ref.shape[2]
    hb = q_ref.shape[1] // LANES
    qi = pl.program_id(2)
    nq = pl.num_programs(2)

    lane = lax.broadcasted_iota(jnp.int32, (tq, LANES), 1)

    def chains(ref):
        out = []
        for a in range(hb):
            q = ref[:, a * LANES:(a + 1) * LANES]
            zero = jnp.zeros_like(q)
            out.append(jnp.where(lane < DIFF_DQK, q, zero))
            out.append(jnp.where(lane >= DIFF_DQK, q, zero))
        return out

    qs = chains(q_ref)
    nc = 2 * hb

    heads = [slice((c // 2) * LANES, (c // 2 + 1) * LANES) for c in range(nc)]

    m_ref[...] = jnp.full_like(m_ref, MASK_VALUE)

    @pl.when((pl.program_id(0) == 0) & (pl.program_id(1) == 0) & (qi == 0))
    def _():
        acc_ref[...] = jnp.zeros_like(acc_ref)

    ones_rows = jnp.ones((L_ROWS, tk), BF16)

    def scores(t, c, masked, q_chains=qs):
        k0 = pl.multiple_of(t * tk, tk)
        s_new = _dot_nt(k_ref[pl.ds(k0, tk), heads[c]], q_chains[c])
        if masked:
            k_chunk = t * (tk // CHUNK) + lax.broadcasted_iota(jnp.int32, (tk, tq), 0) // CHUNK
            q_chunk = qi * (tq // CHUNK) + lax.broadcasted_iota(jnp.int32, (tk, tq), 1) // CHUNK
            s_new = jnp.where(k_chunk <= q_chunk, s_new, MASK_VALUE)
        return s_new

    def keep_scores(c, s_new):
        sb = s_new.astype(BF16)
        s_ref[c] = sb
        mb_ref[c] = jnp.max(sb, axis=0, keepdims=True).astype(F32)

    def softmax_pv(t, c):
        m_prev = m_ref[c]
        m_next = jnp.maximum(m_prev, mb_ref[c])
        alpha = jnp.exp2(m_prev - m_next)
        p = jnp.exp2(s_ref[c] - m_next.astype(BF16))
        m_ref[c] = m_next
        v1 = jnp.concatenate([vt_ref[t, heads[c], :], ones_rows], axis=0)
        return alpha, _dot(v1, p)

    def both(t, masked):
        for c in range(nc):
            alpha, pv = softmax_pv(t - 1, c)
            s_new = scores(t, c, masked)
            acc_ref[c] = alpha * acc_ref[c] + pv
            keep_scores(c, s_new)

    @pl.when(qi == 0)
    def _():
        for c in range(nc):
            keep_scores(c, scores(0, c, True))

    @pl.when(qi > 0)
    def _():
        n_before = qi - 1
        n_quads = n_before // 4

        def quad(i, carry):
            for j in range(4):
                both(4 * i + 1 + j, False)
            return carry
        lax.fori_loop(0, n_quads, quad, 0)
        done = 4 * n_quads

        @pl.when(n_before - done >= 2)
        def _():
            both(done + 1, False)
            both(done + 2, False)

        @pl.when((n_before - done) % 2 == 1)
        def _():
            both(qi - 1, False)

        both(qi, True)

    lam = (jnp.exp(jnp.sum(lq1_ref[...] * lk1_ref[...], axis=1, keepdims=True))
           - jnp.exp(jnp.sum(lq2_ref[...] * lk2_ref[...], axis=1, keepdims=True))
           + lam_init)

    def last_stage(q_next):
        for a in range(hb):
            acc = []
            for c in (2 * a, 2 * a + 1):
                alpha, pv = softmax_pv(qi, c)
                if q_next is not None:
                    s_new = scores(0, c, False, q_next)
                acc.append(alpha * acc_ref[c] + pv)
                if q_next is not None:
                    keep_scores(c, s_new)
            hs = slice(a * LANES, (a + 1) * LANES)
            o0 = acc[0][:DIFF_DV, :] * (1.0 / acc[0][DIFF_DV:DIFF_DV + 1, :])
            o1 = acc[1][:DIFF_DV, :] * (1.0 / acc[1][DIFF_DV:DIFF_DV + 1, :])
            o = o0 - lam * o1
            ms = jnp.mean(o * o, axis=0, keepdims=True)
            o = o * lax.rsqrt(ms + NORM_EPS) * gn_ref[hs, :] * (1.0 - lam_init)
            o_ref[:, hs] = o.T.astype(BF16)

    @pl.when(qi < nq - 1)
    def _():
        last_stage(chains(qn_ref))

    @pl.when(qi == nq - 1)
    def _():
        last_stage(None)


def _diff_attn(dq, dk, dvt, lam_vecs, diff_norm, lam_init, batch, seq):
    tq = TQ_ATT
    tk = dvt.shape[2]
    assert tq == tk
    nq = seq // tq
    hb = HB_ATT
    hv = DIFF_HEADS * DIFF_DV
    vec = pl.BlockSpec((1, DIFF_DQK), lambda b, h, i: (0, 0))
    return pl.pallas_call(
        functools.partial(_diff_body, lam_init),
        out_shape=jax.ShapeDtypeStruct((batch * seq, hv), BF16),
        grid=(batch, DIFF_HEADS // hb, nq),
        in_specs=[
            vec, vec, vec, vec,
            pl.BlockSpec((tq, hb * LANES), lambda b, h, i: (b * nq + i, h)),
            pl.BlockSpec((tq, hb * LANES), lambda b, h, i: (b * nq + jnp.minimum(i + 1, nq - 1), h)),
            pl.BlockSpec((seq, hb * LANES), lambda b, h, i: (b, h)),
            pl.BlockSpec((seq // tk, hb * DIFF_DV, tk), lambda b, h, i: (b, h, 0)),
            pl.BlockSpec((hb * DIFF_DV, 1), lambda b, h, i: (h, 0)),
        ],
        out_specs=pl.BlockSpec((tq, hb * DIFF_DV), lambda b, h, i: (b * nq + i, h)),
        scratch_shapes=[
            pltpu.VMEM((2 * hb, 1, tq), F32),
            pltpu.VMEM((2 * hb, DIFF_DV + L_ROWS, tq), F32),
            pltpu.VMEM((2 * hb, tk, tq), BF16),
            pltpu.VMEM((2 * hb, 1, tq), F32),
        ],
        compiler_params=pltpu.CompilerParams(
            dimension_semantics=("arbitrary", "arbitrary", "arbitrary"),
            vmem_limit_bytes=VMEM_LIMIT_BYTES),
        name="diff_attn",
    )(*lam_vecs, dq, dq, dk, dvt, diff_norm)


def _tail_body(tiles_per_seq, final, x_ref, og_ref, od_ref, p_ref, wo_ref, nf_ref, wu_ref,
               cw_ref, cb_ref, wd_ref, npl_ref, wpg_ref, wpp_ref, nfin_ref,
               y_ref, carry_ref, perm_ref, act_ref):
    tm, d = x_ref.shape
    c = FF_CHUNK
    n_chunks = wd_ref.shape[0] // c
    nseg = SUBLANES
    seg = tm // nseg
    nl = d // LANES

    @pl.when(pl.program_id(0) % tiles_per_seq == 0)
    def _():
        carry_ref[...] = jnp.zeros_like(carry_ref)

    half = og_ref.shape[1]
    h1 = (x_ref[...] + _dot(og_ref[...], wo_ref[0:half, :])
          + _dot(od_ref[...], wo_ref[half:2 * half, :]))
    ple = _dot(p_ref[...].astype(BF16), wpp_ref[...])

    def permute(a):
        for l in range(nl):
            for s in range(nseg):
                perm_ref[l, s * PERM_PITCH:s * PERM_PITCH + seg, :] = (
                    a[s * seg:(s + 1) * seg, l * LANES:(l + 1) * LANES])
        return jnp.concatenate(
            [jnp.concatenate([perm_ref[l, pl.ds(v, nseg, stride=PERM_PITCH), :] for v in range(seg)],
                             axis=0) for l in range(nl)], axis=1)

    def unpermute(a):
        for l in range(nl):
            for v in range(seg):
                perm_ref[l, pl.ds(v, nseg, stride=PERM_PITCH), :] = (
                    a[v * nseg:(v + 1) * nseg, l * LANES:(l + 1) * LANES])
        return jnp.concatenate(
            [jnp.concatenate([perm_ref[l, s * PERM_PITCH:s * PERM_PITCH + seg, :] for s in range(nseg)],
                             axis=0) for l in range(nl)], axis=1)

    u = permute(_rms(h1, nf_ref[...])).astype(BF16)

    d_ff = n_chunks * c

    def cols(ref, j):
        return jnp.concatenate([ref[:, j * c:(j + 1) * c], ref[:, d_ff + j * c:d_ff + (j + 1) * c]], axis=1)

    def up(j):
        return jnp.concatenate([_dot(u, wu_ref[:, j * c:(j + 1) * c]),
                                _dot(u, wu_ref[:, d_ff + j * c:d_ff + (j + 1) * c])], axis=1)

    row = lax.broadcasted_iota(jnp.int32, (SUBLANES, 2 * c), 0)

    def wrapped(last_block, prev_last_block):
        return pltpu.roll(jnp.where(row == SUBLANES - 1, prev_last_block, last_block), 1, 0)

    z_next = up(0)
    for j in range(n_chunks):
        z = z_next
        if j + 1 < n_chunks:
            z_next = up(j + 1)
        prev = carry_ref[j]
        carry_ref[j] = z[tm - 2 * SUBLANES:, :]
        b1 = wrapped(z[tm - SUBLANES:, :], prev[SUBLANES:, :])
        b2 = wrapped(z[tm - 2 * SUBLANES:tm - SUBLANES, :], prev[:SUBLANES, :])
        s1 = jnp.concatenate([b1, z[:tm - SUBLANES, :]], axis=0)
        s2 = jnp.concatenate([b2, b1, z[:tm - 2 * SUBLANES, :]], axis=0)
        cw = cols(cw_ref, j)
        cv = cols(cb_ref, j) + s2 * cw[0:1, :] + s1 * cw[1:2, :] + z * cw[2:3, :]
        gate = cv[:, :c]
        y2 = gate * (GELU_A + GELU_B * (gate * gate))
        act_ref[:, j * c:(j + 1) * c] = (gate / (1.0 + jnp.exp(-y2)) * cv[:, c:]).astype(BF16)

    h2 = h1 + unpermute(_dot(act_ref[...], wd_ref[...]))
    gate = jax.nn.sigmoid(_dot(_rms(h2, npl_ref[...]).astype(BF16), wpg_ref[...]))
    h3 = h2 + gate * ple
    y_ref[...] = _rms(h3, nfin_ref[...]) if final else h3


def _tail(x2, o_gla, o_diff, p2, w_out, norm_ffn, wu, cw, cb, wd, norm_ple, w_pg, w_pp,
          norm_final, seq, final):
    n, d = x2.shape
    tm = TM_OUT
    row = lambda i: (i, 0)
    c2 = lambda i: (0, 0)
    c3 = lambda i: (0, 0, 0)
    one = pl.Buffered(1)

    def resident(a):
        return pl.BlockSpec(a.shape, c3 if a.ndim == 3 else c2, pipeline_mode=one)

    n_chunks = wd.shape[0] // FF_CHUNK
    return pl.pallas_call(
        functools.partial(_tail_body, seq // tm, final),
        out_shape=jax.ShapeDtypeStruct((n, d), F32),
        grid=(n // tm,),
        in_specs=[
            pl.BlockSpec((tm, d), row),
            pl.BlockSpec((tm, o_gla.shape[1]), row),
            pl.BlockSpec((tm, o_diff.shape[1]), row),
            pl.BlockSpec((tm, p2.shape[1]), row),
            resident(w_out), resident(norm_ffn), resident(wu), resident(cw),
            resident(cb), resident(wd), resident(norm_ple), resident(w_pg), resident(w_pp),
            resident(norm_final),
        ],
        out_specs=pl.BlockSpec((tm, d), row),
        scratch_shapes=[
            pltpu.VMEM((n_chunks, 2 * SUBLANES, 2 * FF_CHUNK), F32),
            pltpu.VMEM((d // LANES, SUBLANES * PERM_PITCH, LANES), F32),
            pltpu.VMEM((tm, n_chunks * FF_CHUNK), BF16),
        ],
        compiler_params=pltpu.CompilerParams(
            dimension_semantics=("arbitrary",), vmem_limit_bytes=VMEM_LIMIT_BYTES),
        name="tail",
    )(x2, o_gla, o_diff, p2, w_out, norm_ffn, wu, cw, cb, wd, norm_ple, w_pg, w_pp, norm_final)


def kernel(x, p, positions, norm_mix, w_in, w_a_up, b_a, gla_norm, lam_q1, lam_k1, lam_q2, lam_k2,
           diff_norm, w_out, norm_ffn, w_up, conv_w, conv_b, w_down, norm_ple, w_ple_gate,
           w_ple_proj, norm_final):
    batch, seq, d = x.shape
    depth = w_in.shape[0]
    n = batch * seq
    d_ff = w_down.shape[1]
    assert seq % max(TM_IN, TC_GLA, TQ_ATT, TM_OUT) == 0 and TM_IN % TK_ATT == 0
    assert d_ff % FF_CHUNK == 0 and TK_ATT % CHUNK == 0

    inv_freq = ROPE_THETA ** (-jnp.arange(0, DIFF_DQK, 2, dtype=F32) / DIFF_DQK)
    invf = jnp.tile(inv_freq, LANES // (DIFF_DQK // 2))[None, :]
    sgn = jnp.tile(jnp.concatenate([-jnp.ones((DIFF_DQK // 2,), F32),
                                    jnp.ones((DIFF_DQK // 2,), F32)]), LANES // DIFF_DQK)[None, :]

    h = x.reshape(n, d)
    pos2 = positions.reshape(n, 1)
    for i in range(depth):
        wi = w_in[i]
        ga0 = 1536
        dv0 = ga0 + GLA_GATE_RANK + 1024
        w_in_r = jnp.concatenate(
            [wi[:, :ga0], wi[:, ga0 + GLA_GATE_RANK:dv0], wi[:, ga0:ga0 + GLA_GATE_RANK],
             jnp.zeros((d, LANES - GLA_GATE_RANK), F32)], axis=1).astype(BF16)
        w_dvt = wi[:, dv0:].T.astype(BF16)
        wa_pad = jnp.concatenate(
            [w_a_up[i], jnp.zeros((LANES - GLA_GATE_RANK, w_a_up.shape[2]), F32)], axis=0).astype(BF16)

        gq, gk, gv, gg, la, dq, dk, dvt = _inproj(
            h, pos2, norm_mix[i][None, :], w_in_r, w_dvt, wa_pad, b_a[i][None, :], invf, sgn)
        o_gla = _gla(gq, gk, gv, gg, la, gla_norm[i][None, :], batch, seq)

        lam_init = 0.8 - 0.6 * math.exp(-0.3 * i)
        lam_vecs = [v[i][None, :] for v in (lam_q1, lam_k1, lam_q2, lam_k2)]
        o_diff = _diff_attn(dq, dk, dvt, lam_vecs, diff_norm[i][:, None], lam_init, batch, seq)

        h = _tail(h, o_gla, o_diff, p[i].reshape(n, -1), w_out[i].astype(BF16),
                  norm_ffn[i][None, :], w_up[i].astype(BF16), conv_w[i], conv_b[i][None, :],
                  w_down[i].astype(BF16), norm_ple[i][None, :],
                  w_ple_gate[i].astype(BF16), w_ple_proj[i].astype(BF16),
                  norm_final[None, :], seq, i == depth - 1)
    return h.reshape(batch, seq, d)
```

```python
import functools
import math

import jax
import jax.numpy as jnp
from jax import lax
from jax.experimental import pallas as pl
from jax.experimental.pallas import tpu as pltpu

F32 = jnp.float32
BF16 = jnp.bfloat16

CHUNK = 64
ROPE_THETA = 10000.0
NORM_EPS = 1e-6
GLA_HEADS = 4
GLA_DK = 64
GLA_DV = 128
GLA_GATE_RANK = 16
GLA_GATE_NORM = 16.0
DIFF_HEADS = 4
DIFF_DQK = 64
DIFF_DV = 128
CONV_W = 3
NEG_INF = -1e30
MASK_VALUE = -(2.0 ** 100)
PERM_PITCH = 72
L_ROWS = 16
LOG2E = 1.4426950408889634
GELU_A = 2.0 * math.sqrt(2.0 / math.pi)
GELU_B = GELU_A * 0.044715

LANES = 128
SUBLANES = 8
VMEM_LIMIT_BYTES = 56 * 1024 * 1024

TM_IN = 512
TC_GLA = 512
TQ_ATT = 256
TK_ATT = 256
HB_ATT = 4
TM_OUT = 512
FF_CHUNK = 256


def _rms(x, g):
    return x * lax.rsqrt(jnp.mean(x * x, axis=-1, keepdims=True) + NORM_EPS) * g


def _dot(a, b):
    return jnp.dot(a, b, preferred_element_type=F32)


def _dot_nt(a, b):
    return lax.dot_general(a, b, (((1,), (1,)), ((), ())), preferred_element_type=F32)


def _dot_tn(a, b):
    return lax.dot_general(a, b, (((0,), (0,)), ((), ())), preferred_element_type=F32)


def _inproj_body(x_ref, pos_ref, g_ref, w_ref, wvt_ref, wa_ref, ba_ref, invf_ref, sgn_ref,
                 gq_ref, gk_ref, gv_ref, gg_ref, la_ref, dq_ref, dk_ref, dvt_ref):
    tm = pos_ref.shape[0]
    nfreq = DIFF_DQK // 2
    groups = LANES // nfreq
    assert tm % (LANES * groups) == 0
    pos = pos_ref[...].astype(F32)
    lane_group = lax.broadcasted_iota(jnp.int32, (LANES, LANES), 1) // nfreq

    def spread(packed):
        blocks = []
        for jg in range(groups):
            x = jnp.where(lane_group == jg, packed, 0.0)
            x = x + pltpu.roll(x, 2 * nfreq, 1)
            blocks.append(x + pltpu.roll(x, nfreq, 1))
        return blocks

    cos_blocks, sin_blocks = [], []
    for r0 in range(0, tm, LANES * groups):
        pos_packed = pos[r0:r0 + LANES, :]
        for jg in range(1, groups):
            pos_packed = jnp.where(lane_group == jg,
                                   pos[r0 + jg * LANES:r0 + (jg + 1) * LANES, :], pos_packed)
        ang = pos_packed * invf_ref[...]
        cos_blocks += spread(jnp.cos(ang))
        sin_blocks += spread(jnp.sin(ang))
    cos = jnp.concatenate(cos_blocks, axis=0)
    sin = jnp.concatenate(sin_blocks, axis=0) * sgn_ref[...]
    lane = lax.broadcasted_iota(jnp.int32, (tm, LANES), 1)
    first_half = (lane & nfreq) == 0

    u = _rms(x_ref[...], g_ref[...]).astype(BF16)

    def proj(c0, c1):
        return _dot(u, w_ref[:, c0:c1])

    def rope(z, scale):
        outs = []
        for j in range(z.shape[1] // LANES):
            zj = z[:, j * LANES:(j + 1) * LANES]
            rot = jnp.where(first_half,
                            pltpu.roll(zj, LANES - DIFF_DQK // 2, 1),
                            pltpu.roll(zj, DIFF_DQK // 2, 1))
            outs.append(((zj * cos + rot * sin) * scale).astype(BF16))
        return jnp.concatenate(outs, axis=1)

    gq_ref[...] = (proj(0, 256) * (GLA_DK ** -0.5)).astype(BF16)
    gk_ref[...] = proj(256, 512).astype(BF16)

    dq_ref[...] = rope(proj(1536, 2048), DIFF_DQK ** -0.5 * LOG2E)
    dk_ref[...] = rope(proj(2048, 2560), 1.0)

    a_low = proj(2560, 2688).astype(BF16)
    pre = _dot(a_low, wa_ref[...]) + ba_ref[...]
    log_sig = jnp.minimum(pre, 0.0) - jnp.log1p(jnp.exp(-jnp.abs(pre)))
    la_ref[...] = log_sig * (1.0 / GLA_GATE_NORM)

    gv_ref[...] = proj(512, 1024).astype(BF16)
    gg_ref[...] = proj(1024, 1536).astype(BF16)
    dvt = _dot_nt(wvt_ref[...], u)
    tk = dvt_ref.shape[2]
    for c in range(dvt_ref.shape[0]):
        dvt_ref[c] = dvt[:, c * tk:(c + 1) * tk].astype(BF16)


def _inproj(x2, pos2, norm_mix, w_in_r, w_dvt, wa_pad, b_a, invf, sgn):
    n, d = x2.shape
    tm = TM_IN
    row = lambda i: (i, 0)
    const = lambda i: (0, 0)
    widths = (256, 256, 512, 512, 256, 512, 512)
    dtypes = (BF16, BF16, BF16, BF16, F32, BF16, BF16)
    tk = TK_ATT
    hv = w_dvt.shape[0]
    return pl.pallas_call(
        _inproj_body,
        out_shape=[jax.ShapeDtypeStruct((n, w), dt) for w, dt in zip(widths, dtypes)]
        + [jax.ShapeDtypeStruct((n // tk, hv, tk), BF16)],
        grid=(n // tm,),
        in_specs=[
            pl.BlockSpec((tm, d), row),
            pl.BlockSpec((tm, 1), row),
            pl.BlockSpec((1, d), const),
            pl.BlockSpec(w_in_r.shape, const),
            pl.BlockSpec(w_dvt.shape, const),
            pl.BlockSpec(wa_pad.shape, const),
            pl.BlockSpec((1, 256), const),
            pl.BlockSpec((1, LANES), const),
            pl.BlockSpec((1, LANES), const),
        ],
        out_specs=[pl.BlockSpec((tm, w), row) for w in widths]
        + [pl.BlockSpec((tm // tk, hv, tk), lambda i: (i, 0, 0))],
        compiler_params=pltpu.CompilerParams(
            dimension_semantics=("arbitrary",), vmem_limit_bytes=VMEM_LIMIT_BYTES),
        name="inproj",
    )(x2, pos2, norm_mix, w_in_r, w_dvt, wa_pad, b_a, invf, sgn)


def _gla_body(q_ref, k_ref, v_ref, g_ref, la_ref, gn_ref, o_ref, st_ref):
    @pl.when(pl.program_id(1) == 0)
    def _():
        st_ref[...] = jnp.zeros_like(st_ref)

    hk = GLA_HEADS * GLA_DK
    r_i = lax.broadcasted_iota(jnp.int32, (CHUNK, CHUNK), 0)
    c_i = lax.broadcasted_iota(jnp.int32, (CHUNK, CHUNK), 1)
    tril = (r_i >= c_i).astype(BF16)
    lane_head = lax.broadcasted_iota(jnp.int32, (1, hk), 1) // GLA_DK
    head_mask = [lane_head == h for h in range(GLA_HEADS)]
    rr = lax.broadcasted_iota(jnp.int32, (GLA_HEADS * CHUNK, CHUNK), 0) % CHUNK
    cc = lax.broadcasted_iota(jnp.int32, (GLA_HEADS * CHUNK, CHUNK), 1)
    tri_stack = rr >= cc
    gn = gn_ref[...]

    n_chunks = q_ref.shape[0] // CHUNK
    chunks = range(n_chunks)
    rows_of = [slice(c * CHUNK, (c + 1) * CHUNK) for c in chunks]
    head_rows = [slice(h * CHUNK, (h + 1) * CHUNK) for h in range(GLA_HEADS)]
    head_keys = [slice(h * GLA_DK, (h + 1) * GLA_DK) for h in range(GLA_HEADS)]
    head_vals = [slice(h * GLA_DV, (h + 1) * GLA_DV) for h in range(GLA_HEADS)]

    def cumsum(c):
        la = la_ref[rows_of[c], :]
        la_hi = la.astype(BF16)
        la_lo = (la - la_hi.astype(F32)).astype(BF16)
        return _dot(tril, la_hi) + _dot(tril, la_lo)

    bs = [cumsum(c) for c in chunks]

    def gated(c):
        b = bs[c]
        eb = jnp.exp(b)
        enb = jnp.exp(-b)
        q = q_ref[rows_of[c], :].astype(F32)
        k = k_ref[rows_of[c], :].astype(F32)
        qf = q * eb
        qb = q * enb
        qf_st = jnp.concatenate([jnp.where(m, qf, 0.0) for m in head_mask], axis=0).astype(BF16)
        qb_st = jnp.concatenate([jnp.where(m, qb, 0.0) for m in head_mask], axis=0).astype(BF16)
        k_dec = (k * jnp.exp(b[CHUNK - 1:CHUNK, :] - b)).astype(BF16)
        decay = jnp.exp(b[CHUNK - SUBLANES:, :].T[:, SUBLANES - 1:SUBLANES])
        return qf_st, qb_st, (k * enb).astype(BF16), (k * eb).astype(BF16), k_dec.T, decay

    gs = [gated(c) for c in chunks]
    a_fwd = [_dot_nt(gs[c][0], gs[c][2]) for c in chunks]
    a_bwd = [_dot_nt(gs[c][1], gs[c][3]) for c in chunks]
    a = [jnp.where(tri_stack, a_fwd[c], a_bwd[c]).astype(BF16) for c in chunks]
    o_intra = [[_dot(a[c][head_rows[h], :], v_ref[rows_of[c], head_vals[h]])
                for h in range(GLA_HEADS)] for c in chunks]
    d_st = [jnp.concatenate([_dot(gs[c][4][head_keys[h], :], v_ref[rows_of[c], head_vals[h]])
                             for h in range(GLA_HEADS)], axis=0) for c in chunks]

    st = st_ref[...]
    states = []
    for c in chunks:
        states.append(st.astype(BF16))
        st = st * gs[c][5] + d_st[c]
    st_ref[...] = st

    o_inter = [_dot(gs[c][0], states[c]) for c in chunks]
    for c in chunks:
        rows = rows_of[c]
        for h in range(GLA_HEADS):
            vs = head_vals[h]
            o = o_intra[c][h] + o_inter[c][head_rows[h], :]
            o = _rms(o, gn[:, vs])
            g = g_ref[rows, vs].astype(F32)
            o_ref[rows, vs] = (o * (g * jax.nn.sigmoid(g))).astype(BF16)


def _gla(gq, gk, gv, gg, la, gla_norm, batch, seq):
    tc = TC_GLA
    nt = seq // tc
    row = lambda b, t: (b * nt + t, 0)
    const = lambda b, t: (0, 0)
    hk = GLA_HEADS * GLA_DK
    hv = GLA_HEADS * GLA_DV
    return pl.pallas_call(
        _gla_body,
        out_shape=jax.ShapeDtypeStruct((batch * seq, hv), BF16),
        grid=(batch, nt),
        in_specs=[
            pl.BlockSpec((tc, hk), row),
            pl.BlockSpec((tc, hk), row),
            pl.BlockSpec((tc, hv), row),
            pl.BlockSpec((tc, hv), row),
            pl.BlockSpec((tc, hk), row),
            pl.BlockSpec((1, hv), const),
        ],
        out_specs=pl.BlockSpec((tc, hv), row),
        scratch_shapes=[pltpu.VMEM((hk, GLA_DV), F32)],
        compiler_params=pltpu.CompilerParams(
            dimension_semantics=("arbitrary", "arbitrary"), vmem_limit_bytes=VMEM_LIMIT_BYTES),
        name="gla",
    )(gq, gk, gv, gg, la, gla_norm)


def _diff_body(lam_init, lq1_ref, lk1_ref, lq2_ref, lk2_ref, q_ref, qn_ref, k_ref, vt_ref, gn_ref,
               o_ref, m_ref, acc_ref, s_ref, mb_ref):
    tq = q_ref.shape[0]
    tk = vt_ref.shape[2]
    hb = q_ref.shape[1] // LANES
    qi = pl.program_id(2)
    nq = pl.num_programs(2)

    lane = lax.broadcasted_iota(jnp.int32, (tq, LANES), 1)

    def chains(ref):
        out = []
        for a in range(hb):
            q = ref[:, a * LANES:(a + 1) * LANES]
            zero = jnp.zeros_like(q)
            out.append(jnp.where(lane < DIFF_DQK, q, zero))
            out.append(jnp.where(lane >= DIFF_DQK, q, zero))
        return out

    qs = chains(q_ref)
    nc = 2 * hb

    heads = [slice((c // 2) * LANES, (c // 2 + 1) * LANES) for c in range(nc)]

    m_ref[...] = jnp.full_like(m_ref, MASK_VALUE)

    @pl.when((pl.program_id(0) == 0) & (pl.program_id(1) == 0) & (qi == 0))
    def _():
        acc_ref[...] = jnp.zeros_like(acc_ref)

    ones_rows = jnp.ones((L_ROWS, tk), BF16)

    def scores(t, c, masked, q_chains=qs):
        k0 = pl.multiple_of(t * tk, tk)
        s_new = _dot_nt(k_ref[pl.ds(k0, tk), heads[c]], q_chains[c])
        if masked:
            k_chunk = t * (tk // CHUNK) + lax.broadcasted_iota(jnp.int32, (tk, tq), 0) // CHUNK
            q_chunk = qi * (tq // CHUNK) + lax.broadcasted_iota(jnp.int32, (tk, tq), 1) // CHUNK
            s_new = jnp.where(k_chunk <= q_chunk, s_new, MASK_VALUE)
        return s_new

    def keep_scores(c, s_new):
        sb = s_new.astype(BF16)
        s_ref[c] = sb
        mb_ref[c] = jnp.max(sb, axis=0, keepdims=True).astype(F32)

    def softmax_pv(t, c):
        m_prev = m_ref[c]
        m_next = jnp.maximum(m_prev, mb_ref[c])
        alpha = jnp.exp2(m_prev - m_next)
        p = jnp.exp2(s_ref[c] - m_next.astype(BF16))
        m_ref[c] = m_next
        v1 = jnp.concatenate([vt_ref[t, heads[c], :], ones_rows], axis=0)
        return alpha, _dot(v1, p)

    def both(t, masked):
        for c in range(nc):
            alpha, pv = softmax_pv(t - 1, c)
            s_new = scores(t, c, masked)
            acc_ref[c] = alpha * acc_ref[c] + pv
            keep_scores(c, s_new)

    @pl.when(qi == 0)
    def _():
        for c in range(nc):
            keep_scores(c, scores(0, c, True))

    @pl.when(qi > 0)
    def _():
        n_before = qi - 1
        n_quads = n_before // 4

        def quad(i, carry):
            for j in range(4):
                both(4 * i + 1 + j, False)
            return carry
        lax.fori_loop(0, n_quads, quad, 0)
        done = 4 * n_quads

        for left in range(4):
            @pl.when(n_before - done == left)
            def _(left=left):
                for j in range(left):
                    both(done + 1 + j, False)
                both(qi, True)

    lam = (jnp.exp(jnp.sum(lq1_ref[...] * lk1_ref[...], axis=1, keepdims=True))
           - jnp.exp(jnp.sum(lq2_ref[...] * lk2_ref[...], axis=1, keepdims=True))
           + lam_init)

    def last_stage(q_next):
        for a in range(hb):
            acc = []
            for c in (2 * a, 2 * a + 1):
                alpha, pv = softmax_pv(qi, c)
                if q_next is not None:
                    s_new = scores(0, c, False, q_next)
                acc.append(alpha * acc_ref[c] + pv)
                if q_next is not None:
                    keep_scores(c, s_new)
            hs = slice(a * LANES, (a + 1) * LANES)
            o0 = acc[0][:DIFF_DV, :] * (1.0 / acc[0][DIFF_DV:DIFF_DV + 1, :])
            o1 = acc[1][:DIFF_DV, :] * (1.0 / acc[1][DIFF_DV:DIFF_DV + 1, :])
            o = o0 - lam * o1
            ms = jnp.mean(o * o, axis=0, keepdims=True)
            o = o * lax.rsqrt(ms + NORM_EPS) * gn_ref[hs, :] * (1.0 - lam_init)
            o_ref[:, hs] = o.T.astype(BF16)

    @pl.when(qi < nq - 1)
    def _():
        last_stage(chains(qn_ref))

    @pl.when(qi == nq - 1)
    def _():
        last_stage(None)


def _diff_attn(dq, dk, dvt, lam_vecs, diff_norm, lam_init, batch, seq):
    tq = TQ_ATT
    tk = dvt.shape[2]
    assert tq == tk
    nq = seq // tq
    hb = HB_ATT
    hv = DIFF_HEADS * DIFF_DV
    vec = pl.BlockSpec((1, DIFF_DQK), lambda b, h, i: (0, 0))
    return pl.pallas_call(
        functools.partial(_diff_body, lam_init),
        out_shape=jax.ShapeDtypeStruct((batch * seq, hv), BF16),
        grid=(batch, DIFF_HEADS // hb, nq),
        in_specs=[
            vec, vec, vec, vec,
            pl.BlockSpec((tq, hb * LANES), lambda b, h, i: (b * nq + i, h)),
            pl.BlockSpec((tq, hb * LANES), lambda b, h, i: (b * nq + jnp.minimum(i + 1, nq - 1), h)),
            pl.BlockSpec((seq, hb * LANES), lambda b, h, i: (b, h)),
            pl.BlockSpec((seq // tk, hb * DIFF_DV, tk), lambda b, h, i: (b, h, 0)),
            pl.BlockSpec((hb * DIFF_DV, 1), lambda b, h, i: (h, 0)),
        ],
        out_specs=pl.BlockSpec((tq, hb * DIFF_DV), lambda b, h, i: (b * nq + i, h)),
        scratch_shapes=[
            pltpu.VMEM((2 * hb, 1, tq), F32),
            pltpu.VMEM((2 * hb, DIFF_DV + L_ROWS, tq), F32),
            pltpu.VMEM((2 * hb, tk, tq), BF16),
            pltpu.VMEM((2 * hb, 1, tq), F32),
        ],
        compiler_params=pltpu.CompilerParams(
            dimension_semantics=("arbitrary", "arbitrary", "arbitrary"),
            vmem_limit_bytes=VMEM_LIMIT_BYTES),
        name="diff_attn",
    )(*lam_vecs, dq, dq, dk, dvt, diff_norm)


def _tail_body(tiles_per_seq, final, x_ref, og_ref, od_ref, p_ref, wo_ref, nf_ref, wu_ref,
               cw_ref, cb_ref, wd_ref, npl_ref, wpg_ref, wpp_ref, nfin_ref,
               y_ref, carry_ref, perm_ref, act_ref):
    tm, d = x_ref.shape
    c = FF_CHUNK
    n_chunks = wd_ref.shape[0] // c
    nseg = SUBLANES
    seg = tm // nseg
    nl = d // LANES

    @pl.when(pl.program_id(0) % tiles_per_seq == 0)
    def _():
        carry_ref[...] = jnp.zeros_like(carry_ref)

    half = og_ref.shape[1]
    h1 = (x_ref[...] + _dot(og_ref[...], wo_ref[0:half, :])
          + _dot(od_ref[...], wo_ref[half:2 * half, :]))
    ple = _dot(p_ref[...].astype(BF16), wpp_ref[...])

    def permute(a):
        for l in range(nl):
            for s in range(nseg):
                perm_ref[l, s * PERM_PITCH:s * PERM_PITCH + seg, :] = (
                    a[s * seg:(s + 1) * seg, l * LANES:(l + 1) * LANES])
        return jnp.concatenate(
            [jnp.concatenate([perm_ref[l, pl.ds(v, nseg, stride=PERM_PITCH), :] for v in range(seg)],
                             axis=0) for l in range(nl)], axis=1)

    def unpermute(a):
        for l in range(nl):
            for v in range(seg):
                perm_ref[l, pl.ds(v, nseg, stride=PERM_PITCH), :] = (
                    a[v * nseg:(v + 1) * nseg, l * LANES:(l + 1) * LANES])
        return jnp.concatenate(
            [jnp.concatenate([perm_ref[l, s * PERM_PITCH:s * PERM_PITCH + seg, :] for s in range(nseg)],
                             axis=0) for l in range(nl)], axis=1)

    u = permute(_rms(h1, nf_ref[...])).astype(BF16)

    d_ff = n_chunks * c

    def cols(ref, j):
        return jnp.concatenate([ref[:, j * c:(j + 1) * c], ref[:, d_ff + j * c:d_ff + (j + 1) * c]], axis=1)

    def up(j):
        return jnp.concatenate([_dot(u, wu_ref[:, j * c:(j + 1) * c]),
                                _dot(u, wu_ref[:, d_ff + j * c:d_ff + (j + 1) * c])], axis=1)

    row = lax.broadcasted_iota(jnp.int32, (SUBLANES, 2 * c), 0)

    def wrapped(last_block, prev_last_block):
        return pltpu.roll(jnp.where(row == SUBLANES - 1, prev_last_block, last_block), 1, 0)

    z_next = up(0)
    for j in range(n_chunks):
        z = z_next
        if j + 1 < n_chunks:
            z_next = up(j + 1)
        prev = carry_ref[j]
        carry_ref[j] = z[tm - 2 * SUBLANES:, :]
        b1 = wrapped(z[tm - SUBLANES:, :], prev[SUBLANES:, :])
        b2 = wrapped(z[tm - 2 * SUBLANES:tm - SUBLANES, :], prev[:SUBLANES, :])
        s1 = jnp.concatenate([b1, z[:tm - SUBLANES, :]], axis=0)
        s2 = jnp.concatenate([b2, b1, z[:tm - 2 * SUBLANES, :]], axis=0)
        cw = cols(cw_ref, j)
        cv = cols(cb_ref, j) + s2 * cw[0:1, :] + s1 * cw[1:2, :] + z * cw[2:3, :]
        gate = cv[:, :c]
        y2 = gate * (GELU_A + GELU_B * (gate * gate))
        act_ref[:, j * c:(j + 1) * c] = (gate / (1.0 + jnp.exp(-y2)) * cv[:, c:]).astype(BF16)

    h2 = h1 + unpermute(_dot(act_ref[...], wd_ref[...]))
    gate = jax.nn.sigmoid(_dot(_rms(h2, npl_ref[...]).astype(BF16), wpg_ref[...]))
    h3 = h2 + gate * ple
    y_ref[...] = _rms(h3, nfin_ref[...]) if final else h3


def _tail(x2, o_gla, o_diff, p2, w_out, norm_ffn, wu, cw, cb, wd, norm_ple, w_pg, w_pp,
          norm_final, seq, final):
    n, d = x2.shape
    tm = TM_OUT
    row = lambda i: (i, 0)
    c2 = lambda i: (0, 0)
    c3 = lambda i: (0, 0, 0)
    one = pl.Buffered(1)

    def resident(a):
        return pl.BlockSpec(a.shape, c3 if a.ndim == 3 else c2, pipeline_mode=one)

    n_chunks = wd.shape[0] // FF_CHUNK
    return pl.pallas_call(
        functools.partial(_tail_body, seq // tm, final),
        out_shape=jax.ShapeDtypeStruct((n, d), F32),
        grid=(n // tm,),
        in_specs=[
            pl.BlockSpec((tm, d), row),
            pl.BlockSpec((tm, o_gla.shape[1]), row),
            pl.BlockSpec((tm, o_diff.shape[1]), row),
            pl.BlockSpec((tm, p2.shape[1]), row),
            resident(w_out), resident(norm_ffn), resident(wu), resident(cw),
            resident(cb), resident(wd), resident(norm_ple), resident(w_pg), resident(w_pp),
            resident(norm_final),
        ],
        out_specs=pl.BlockSpec((tm, d), row),
        scratch_shapes=[
            pltpu.VMEM((n_chunks, 2 * SUBLANES, 2 * FF_CHUNK), F32),
            pltpu.VMEM((d // LANES, SUBLANES * PERM_PITCH, LANES), F32),
            pltpu.VMEM((tm, n_chunks * FF_CHUNK), BF16),
        ],
        compiler_params=pltpu.CompilerParams(
            dimension_semantics=("arbitrary",), vmem_limit_bytes=VMEM_LIMIT_BYTES),
        name="tail",
    )(x2, o_gla, o_diff, p2, w_out, norm_ffn, wu, cw, cb, wd, norm_ple, w_pg, w_pp, norm_final)


def kernel(x, p, positions, norm_mix, w_in, w_a_up, b_a, gla_norm, lam_q1, lam_k1, lam_q2, lam_k2,
           diff_norm, w_out, norm_ffn, w_up, conv_w, conv_b, w_down, norm_ple, w_ple_gate,
           w_ple_proj, norm_final):
    batch, seq, d = x.shape
    depth = w_in.shape[0]
    n = batch * seq
    d_ff = w_down.shape[1]
    assert seq % max(TM_IN, TC_GLA, TQ_ATT, TM_OUT) == 0 and TM_IN % TK_ATT == 0
    assert d_ff % FF_CHUNK == 0 and TK_ATT % CHUNK == 0

    inv_freq = ROPE_THETA ** (-jnp.arange(0, DIFF_DQK, 2, dtype=F32) / DIFF_DQK)
    invf = jnp.tile(inv_freq, LANES // (DIFF_DQK // 2))[None, :]
    sgn = jnp.tile(jnp.concatenate([-jnp.ones((DIFF_DQK // 2,), F32),
                                    jnp.ones((DIFF_DQK // 2,), F32)]), LANES // DIFF_DQK)[None, :]

    h = x.reshape(n, d)
    pos2 = positions.reshape(n, 1)
    for i in range(depth):
        wi = w_in[i]
        ga0 = 1536
        dv0 = ga0 + GLA_GATE_RANK + 1024
        w_in_r = jnp.concatenate(
            [wi[:, :ga0], wi[:, ga0 + GLA_GATE_RANK:dv0], wi[:, ga0:ga0 + GLA_GATE_RANK],
             jnp.zeros((d, LANES - GLA_GATE_RANK), F32)], axis=1).astype(BF16)
        w_dvt = wi[:, dv0:].T.astype(BF16)
        wa_pad = jnp.concatenate(
            [w_a_up[i], jnp.zeros((LANES - GLA_GATE_RANK, w_a_up.shape[2]), F32)], axis=0).astype(BF16)

        gq, gk, gv, gg, la, dq, dk, dvt = _inproj(
            h, pos2, norm_mix[i][None, :], w_in_r, w_dvt, wa_pad, b_a[i][None, :], invf, sgn)
        o_gla = _gla(gq, gk, gv, gg, la, gla_norm[i][None, :], batch, seq)

        lam_init = 0.8 - 0.6 * math.exp(-0.3 * i)
        lam_vecs = [v[i][None, :] for v in (lam_q1, lam_k1, lam_q2, lam_k2)]
        o_diff = _diff_attn(dq, dk, dvt, lam_vecs, diff_norm[i][:, None], lam_init, batch, seq)

        h = _tail(h, o_gla, o_diff, p[i].reshape(n, -1), w_out[i].astype(BF16),
                  norm_ffn[i][None, :], w_up[i].astype(BF16), conv_w[i], conv_b[i][None, :],
                  w_down[i].astype(BF16), norm_ple[i][None, :],
                  w_ple_gate[i].astype(BF16), w_ple_proj[i].astype(BF16),
                  norm_final[None, :], seq, i == depth - 1)
    return h.reshape(batch, seq, d)
```
